```python
import jax, jax.numpy as jnp
from jax import lax
import numpy as np

D_MODEL = 1024
BATCH = 16
SEQ = 2048
DEPTH = 1

CHUNK = 64
SB_HEADS = 8
SB_HEAD_DIM = 64
SB_WIDTH = SB_HEADS * SB_HEAD_DIM
SG_GROUPS = 8
SG_GROUP_DIM = 64
SG_WIDTH = SG_GROUPS * SG_GROUP_DIM
SG_BLOCK = 128
Q_BLOCK = 128
MIX_WIDTH = SB_WIDTH + SG_WIDTH
IN_WIDTH = 3 * SB_WIDTH + 2 * SG_WIDTH
D_FF = 4 * D_MODEL
EPS = 1e-6

kernel_name = "hybrid_stickbreak_gmlp_block"


def rmsnorm(x, g):
    xf = x.astype(jnp.float32)
    y = xf * lax.rsqrt(jnp.mean(xf * xf, axis=-1, keepdims=True) + EPS)
    return (y * g.astype(jnp.float32)).astype(x.dtype)


def layernorm(x, g, b):
    xf = x.astype(jnp.float32)
    mu = jnp.mean(xf, axis=-1, keepdims=True)
    xc = xf - mu
    var = jnp.mean(xc * xc, axis=-1, keepdims=True)
    y = xc * lax.rsqrt(var + EPS) * g.astype(jnp.float32) + b.astype(jnp.float32)
    return y.astype(x.dtype)


def stick_breaking_attention(q, k, v):
    b, s, h, dh = q.shape
    scale = 1.0 / np.sqrt(dh).astype(np.float32)
    qh = jnp.transpose(q, (0, 2, 1, 3))
    kh = jnp.transpose(k, (0, 2, 1, 3))
    vh = jnp.transpose(v, (0, 2, 1, 3))
    outs = []
    for blk in range(s // Q_BLOCK):
        q0 = blk * Q_BLOCK
        kv_len = q0 + Q_BLOCK
        qb = qh[:, :, q0:kv_len]
        kb = kh[:, :, :kv_len]
        vb = vh[:, :, :kv_len]
        z = jnp.einsum('bhqd,bhkd->bhqk', qb, kb).astype(jnp.float32) * scale
        t_idx = q0 + jnp.arange(Q_BLOCK)[:, None]
        s_idx = jnp.arange(kv_len)[None, :]
        causal = s_idx < t_idx
        log_beta = jax.nn.log_sigmoid(z)
        log_keep = jnp.where(causal, jax.nn.log_sigmoid(-z), 0.0)
        suffix = lax.cumsum(log_keep, axis=3, reverse=True) - log_keep
        a = jnp.where(causal, jnp.exp(log_beta + suffix), 0.0)
        outs.append(jnp.einsum('bhqk,bhkd->bhqd', a.astype(vb.dtype), vb))
    o = jnp.concatenate(outs, axis=2)
    return jnp.transpose(o, (0, 2, 1, 3)).reshape(b, s, h * dh)


def spatial_gating(u, vg, ln_g, ln_b, w_s, b_s):
    b, s, _ = u.shape
    vn = layernorm(vg, ln_g, ln_b)
    vr = vn.reshape(b, s // SG_BLOCK, SG_BLOCK, SG_GROUPS, SG_GROUP_DIM)
    pos = jnp.arange(SG_BLOCK)
    mask = (pos[None, :] // CHUNK) <= (pos[:, None] // CHUNK)
    w = jnp.where(mask[None], w_s, 0.0).astype(vr.dtype)
    mixed = jnp.einsum('gts,bnsgc->bntgc', w, vr) + jnp.transpose(b_s)[None, None, :, :, None]
    return u * mixed.reshape(b, s, SG_WIDTH)


def setup_inputs(seed: int = 0) -> dict:
    key = jax.random.key(seed)
    ks = jax.random.split(key, 14)
    f32 = jnp.float32
    x = jax.random.normal(ks[0], (BATCH, SEQ, D_MODEL), f32)
    norm_mix_g = 1.0 + 0.02 * jax.random.normal(ks[1], (DEPTH, D_MODEL), f32)
    w_in = jax.random.normal(ks[2], (DEPTH, D_MODEL, IN_WIDTH), f32) * D_MODEL ** -0.5
    sg_ln_g = 1.0 + 0.02 * jax.random.normal(ks[3], (DEPTH, SG_WIDTH), f32)
    sg_ln_b = 0.02 * jax.random.normal(ks[4], (DEPTH, SG_WIDTH), f32)
    sg_w = jax.random.normal(ks[5], (DEPTH, SG_GROUPS, SG_BLOCK, SG_BLOCK), f32) * SG_BLOCK ** -0.5
    sg_b = 1.0 + 0.02 * jax.random.normal(ks[6], (DEPTH, SG_GROUPS, SG_BLOCK), f32)
    out_norm_g = 1.0 + 0.02 * jax.random.normal(ks[7], (DEPTH, MIX_WIDTH), f32)
    w_out = jax.random.normal(ks[8], (DEPTH, MIX_WIDTH, D_MODEL), f32) * MIX_WIDTH ** -0.5
    norm_mlp_g = 1.0 + 0.02 * jax.random.normal(ks[9], (DEPTH, D_MODEL), f32)
    w_up = jax.random.normal(ks[10], (DEPTH, D_MODEL, D_FF), f32) * D_MODEL ** -0.5
    w_down = jax.random.normal(ks[11], (DEPTH, D_FF, D_MODEL), f32) * D_FF ** -0.5
    norm_final_g = 1.0 + 0.02 * jax.random.normal(ks[12], (D_MODEL,), f32)
    return {"x": x, "norm_mix_g": norm_mix_g, "w_in": w_in, "sg_ln_g": sg_ln_g,
            "sg_ln_b": sg_ln_b, "sg_w": sg_w, "sg_b": sg_b, "out_norm_g": out_norm_g,
            "w_out": w_out, "norm_mlp_g": norm_mlp_g, "w_up": w_up, "w_down": w_down,
            "norm_final_g": norm_final_g}


def reference(x, norm_mix_g, w_in, sg_ln_g, sg_ln_b, sg_w, sg_b, out_norm_g, w_out,
              norm_mlp_g, w_up, w_down, norm_final_g):
    b, s, _ = x.shape
    split_at = [SB_WIDTH, 2 * SB_WIDTH, 3 * SB_WIDTH, 3 * SB_WIDTH + SG_WIDTH]
    for layer in range(DEPTH):
        h = rmsnorm(x, norm_mix_g[layer])
        proj = jnp.einsum('bsd,de->bse', h, w_in[layer])
        q, k, v, u, vg = jnp.split(proj, split_at, axis=-1)
        a_out = stick_breaking_attention(q.reshape(b, s, SB_HEADS, SB_HEAD_DIM),
                                         k.reshape(b, s, SB_HEADS, SB_HEAD_DIM),
                                         v.reshape(b, s, SB_HEADS, SB_HEAD_DIM))
        b_out = spatial_gating(jax.nn.gelu(u), jax.nn.gelu(vg), sg_ln_g[layer], sg_ln_b[layer],
                               sg_w[layer], sg_b[layer])
        a_out = rmsnorm(a_out, out_norm_g[layer, :SB_WIDTH])
        b_out = rmsnorm(b_out, out_norm_g[layer, SB_WIDTH:])
        mix = jnp.concatenate([a_out, b_out], axis=-1)
        x = x + jnp.einsum('bse,ed->bsd', mix, w_out[layer])
        h = rmsnorm(x, norm_mlp_g[layer])
        act = jnp.square(jax.nn.relu(jnp.einsum('bsd,df->bsf', h, w_up[layer])))
        x = x + jnp.einsum('bsf,fd->bsd', act, w_down[layer])
    return rmsnorm(x, norm_final_g)
```

```python
import functools

import jax
import jax.numpy as jnp
from jax import lax
from jax.experimental import pallas as pl
from jax.experimental.pallas import tpu as pltpu

D_MODEL = 1024
CHUNK = 64
SB_HEADS = 8
SB_HEAD_DIM = 64
SB_WIDTH = SB_HEADS * SB_HEAD_DIM
SG_GROUPS = 8
SG_GROUP_DIM = 64
SG_WIDTH = SG_GROUPS * SG_GROUP_DIM
SG_BLOCK = 128
MIX_WIDTH = SB_WIDTH + SG_WIDTH
IN_WIDTH = 3 * SB_WIDTH + 2 * SG_WIDTH
D_FF = 4 * D_MODEL
EPS = 1e-6

LANES = 128
ATT_BLOCK = 128
KEY_GROUP_LOG2 = 2
KEY_GROUP = 1 << KEY_GROUP_LOG2
HEAD_PAIRS = SB_HEADS * SB_HEAD_DIM // LANES
VMEM_LIMIT_BYTES = 56 * 1024 * 1024

F32 = jnp.float32
BF16 = jnp.bfloat16


def _rms(x, g):
    ms = jnp.mean(x * x, axis=-1, keepdims=True)
    return x * lax.rsqrt(ms + EPS) * g


def _inproj_kernel(x_ref, g_ref, w_ref, o_ref):
    h = _rms(x_ref[...], g_ref[...]).astype(BF16)
    o_ref[...] = jnp.dot(h, w_ref[...], preferred_element_type=F32).astype(o_ref.dtype)


def _inproj(x2d, g, w_bf16, tm):
    n = x2d.shape[0]
    return pl.pallas_call(
        _inproj_kernel,
        grid=(n // tm,),
        in_specs=[
            pl.BlockSpec((tm, D_MODEL), lambda i: (i, 0)),
            pl.BlockSpec((1, D_MODEL), lambda i: (0, 0)),
            pl.BlockSpec((D_MODEL, IN_WIDTH), lambda i: (0, 0), pipeline_mode=pl.Buffered(1)),
        ],
        out_specs=pl.BlockSpec((tm, IN_WIDTH), lambda i: (i, 0)),
        out_shape=jax.ShapeDtypeStruct((n, IN_WIDTH), BF16),
        compiler_params=pltpu.CompilerParams(
            dimension_semantics=("arbitrary",), vmem_limit_bytes=VMEM_LIMIT_BYTES),
        name="inproj",
    )(x2d, g, w_bf16)


def _split_bf16(x):
    hi = x.astype(BF16)
    lo = (x - hi.astype(F32)).astype(BF16)
    return hi, lo


def _attn_kernel(q_ref, k_ref, v_ref, o_ref, acc_ref, carry_ref):
    s_len = q_ref.shape[1]
    nblk = s_len // ATT_BLOCK
    scale = 1.0 / (SB_HEAD_DIM ** 0.5)
    blk = ATT_BLOCK

    row2 = lax.broadcasted_iota(jnp.int32, (2 * blk, blk), 0)
    col2 = lax.broadcasted_iota(jnp.int32, (2 * blk, blk), 1)
    causal = col2 < (row2 & (blk - 1))
    krow = lax.broadcasted_iota(jnp.int32, (blk, 2 * blk), 0)
    kcol = lax.broadcasted_iota(jnp.int32, (blk, 2 * blk), 1)
    u_half = jnp.where((krow > kcol) | (kcol >= blk), 1.0, 0.0).astype(BF16)
    u_mat = jnp.concatenate([u_half, u_half], axis=0)

    def group(qm2, kblk0, nb, diag):
        koff = pl.multiple_of(kblk0 * blk, blk)
        k = k_ref[0, pl.ds(koff, nb * blk), :]
        v = v_ref[0, pl.ds(koff, nb * blk), :]
        z = lax.dot_general(qm2, k, (((1,), (1,)), ((), ())), preferred_element_type=F32)
        sp = jnp.maximum(z, 0.0) + jnp.log(1.0 + jnp.exp(-jnp.abs(z)))
        log_beta = z - sp
        xs = []
        for b in range(nb):
            log_keep = -sp[:, b * blk:(b + 1) * blk]
            if diag and b == nb - 1:
                log_keep = jnp.where(causal, log_keep, 0.0)
            hi, lo = _split_bf16(log_keep)
            xs.append(jnp.concatenate([hi, lo], axis=1))
        st = jnp.dot(jnp.concatenate(xs, axis=0), u_mat, preferred_element_type=F32)
        carry = carry_ref[...]
        a_blocks = [None] * nb
        for b in reversed(range(nb)):
            st_b = st[b * 2 * blk:(b + 1) * 2 * blk]
            a = jnp.exp(log_beta[:, b * blk:(b + 1) * blk] + (st_b[:, :blk] + carry))
            if diag and b == nb - 1:
                a = jnp.where(causal, a, 0.0)
            a_blocks[b] = a.astype(BF16)
            carry = carry + st_b[:, blk:]
        carry_ref[...] = carry
        acc_ref[...] += jnp.dot(jnp.concatenate(a_blocks, axis=1), v, preferred_element_type=F32)

    lane2 = lax.broadcasted_iota(jnp.int32, (2 * blk, LANES), 1)
    head_lanes = (lane2 < SB_HEAD_DIM) == (row2 < blk)
    out_lane = lax.broadcasted_iota(jnp.int32, (blk, LANES), 1)

    def q_block(i, _):
        qoff = pl.multiple_of(i * blk, blk)
        q = q_ref[0, pl.ds(qoff, blk), :] * jnp.asarray(scale, BF16)
        q2 = jnp.concatenate([q, q], axis=0)
        qm2 = jnp.where(head_lanes, q2, jnp.zeros_like(q2))
        acc_ref[...] = jnp.zeros_like(acc_ref)
        carry_ref[...] = jnp.zeros_like(carry_ref)
        full = lax.shift_right_logical(i, KEY_GROUP_LOG2)
        rem = i & (KEY_GROUP - 1)
        for r in range(KEY_GROUP):
            @pl.when(rem == r)
            def _():
                group(qm2, full * KEY_GROUP, r + 1, True)

        def k_group(t, _):
            group(qm2, (full - 1 - t) * KEY_GROUP, KEY_GROUP, False)
            return 0

        lax.fori_loop(0, full, k_group, 0)
        acc = acc_ref[...]
        o_ref[0, pl.ds(qoff, blk), :] = jnp.where(out_lane < SB_HEAD_DIM, acc[:blk], acc[blk:])
        return 0

    lax.fori_loop(0, nblk, q_block, 0)


def _attention(proj3d):
    b, s, _ = proj3d.shape
    spec = lambda off: pl.BlockSpec((1, s, LANES), lambda bi, p: (bi, 0, off + p))
    return pl.pallas_call(
        _attn_kernel,
        grid=(b, HEAD_PAIRS),
        in_specs=[spec(0), spec(HEAD_PAIRS), spec(2 * HEAD_PAIRS)],
        out_specs=pl.BlockSpec((1, s, LANES), lambda bi, p: (bi, 0, p)),
        out_shape=jax.ShapeDtypeStruct((b, s, SB_WIDTH), F32),
        scratch_shapes=[
            pltpu.VMEM((2 * ATT_BLOCK, LANES), F32),
            pltpu.VMEM((2 * ATT_BLOCK, ATT_BLOCK), F32),
        ],
        compiler_params=pltpu.CompilerParams(
            dimension_semantics=("arbitrary", "arbitrary"), vmem_limit_bytes=VMEM_LIMIT_BYTES),
        name="stickbreak_attn",
    )(proj3d, proj3d, proj3d)


def _mix_mlp_kernel(x_ref, a_ref, u_ref, vg_ref, lng_ref, lnb_ref, sgw_ref, sgb_ref,
                    ong_ref, wout_ref, nmg_ref, wup_ref, wdn_ref, nfg_ref, o_ref,
                    mixed_ref):
    tm = x_ref.shape[0]
    vact = jax.nn.gelu(vg_ref[...].astype(F32))
    mu = jnp.mean(vact, axis=-1, keepdims=True)
    vc = vact - mu
    var = jnp.mean(vc * vc, axis=-1, keepdims=True)
    vn = (vc * lax.rsqrt(var + EPS) * lng_ref[...] + lnb_ref[...]).astype(BF16)

    pos_t = lax.broadcasted_iota(jnp.int32, (SG_BLOCK, SG_BLOCK), 0)
    pos_s = lax.broadcasted_iota(jnp.int32, (SG_BLOCK, SG_BLOCK), 1)
    chunk_mask = (pos_s // CHUNK) <= (pos_t // CHUNK)
    lane = lax.broadcasted_iota(jnp.int32, (SG_BLOCK, LANES), 1)
    first = lane < SG_GROUP_DIM
    zero = jnp.zeros((SG_BLOCK, LANES), BF16)
    for pair in range(SG_GROUPS // 2):
        w0 = jnp.where(chunk_mask, sgw_ref[2 * pair], 0.0).astype(BF16)
        w1 = jnp.where(chunk_mask, sgw_ref[2 * pair + 1], 0.0).astype(BF16)
        w_pair = jnp.concatenate([w0, w1], axis=1)
        for r in range(tm // SG_BLOCK):
            vblk = vn[r * SG_BLOCK:(r + 1) * SG_BLOCK, pair * LANES:(pair + 1) * LANES]
            rhs = jnp.concatenate([jnp.where(first, vblk, zero), jnp.where(first, zero, vblk)], axis=0)
            mixed_ref[r * SG_BLOCK:(r + 1) * SG_BLOCK, pair * LANES:(pair + 1) * LANES] = (
                jnp.dot(w_pair, rhs, preferred_element_type=F32)
                + sgb_ref[:, pair * LANES:(pair + 1) * LANES])
    b_out = jax.nn.gelu(u_ref[...].astype(F32)) * mixed_ref[...]

    ong = ong_ref[...]
    a_n = _rms(a_ref[...], ong[:, :SB_WIDTH]).astype(BF16)
    b_n = _rms(b_out, ong[:, SB_WIDTH:]).astype(BF16)
    mix = jnp.concatenate([a_n, b_n], axis=1)
    x1 = x_ref[...] + jnp.dot(mix, wout_ref[...], preferred_element_type=F32)

    h = _rms(x1, nmg_ref[...]).astype(BF16)
    y = x1
    fc = D_MODEL
    for c in range(D_FF // fc):
        up = jnp.dot(h, wup_ref[:, c * fc:(c + 1) * fc], preferred_element_type=F32)
        act = jnp.square(jnp.maximum(up, 0.0)).astype(BF16)
        y = y + jnp.dot(act, wdn_ref[c * fc:(c + 1) * fc, :], preferred_element_type=F32)
    o_ref[...] = _rms(y, nfg_ref[...])


def _mix_mlp(x2d, a2d, proj2d, lng, lnb, sgw, sgb_full, ong, wout, nmg, wup, wdn, nfg, tm):
    n = x2d.shape[0]
    const = lambda shape: pl.BlockSpec(shape, lambda i: (0,) * len(shape),
                                       pipeline_mode=pl.Buffered(1))
    u_blk = 3 * SB_WIDTH // SG_WIDTH
    return pl.pallas_call(
        _mix_mlp_kernel,
        grid=(n // tm,),
        in_specs=[
            pl.BlockSpec((tm, D_MODEL), lambda i: (i, 0)),
            pl.BlockSpec((tm, SB_WIDTH), lambda i: (i, 0)),
            pl.BlockSpec((tm, SG_WIDTH), lambda i: (i, u_blk)),
            pl.BlockSpec((tm, SG_WIDTH), lambda i: (i, u_blk + 1)),
            const((1, SG_WIDTH)), const((1, SG_WIDTH)),
            const((SG_GROUPS, SG_BLOCK, SG_BLOCK)), const((SG_BLOCK, SG_WIDTH)),
            const((1, MIX_WIDTH)), const((MIX_WIDTH, D_MODEL)),
            const((1, D_MODEL)), const((D_MODEL, D_FF)), const((D_FF, D_MODEL)),
            const((1, D_MODEL)),
        ],
        out_specs=pl.BlockSpec((tm, D_MODEL), lambda i: (i, 0)),
        out_shape=jax.ShapeDtypeStruct((n, D_MODEL), F32),
        scratch_shapes=[pltpu.VMEM((tm, SG_WIDTH), F32)],
        compiler_params=pltpu.CompilerParams(
            dimension_semantics=("arbitrary",), vmem_limit_bytes=VMEM_LIMIT_BYTES),
        name="mix_mlp",
    )(x2d, a2d, proj2d, proj2d, lng, lnb, sgw, sgb_full, ong, wout, nmg, wup, wdn, nfg)


def kernel(x, norm_mix_g, w_in, sg_ln_g, sg_ln_b, sg_w, sg_b, out_norm_g, w_out,
           norm_mlp_g, w_up, w_down, norm_final_g):
    b, s, d = x.shape
    n = b * s
    assert w_in.shape[0] == 1, "the final norm is fused into the (single) layer's last kernel"
    layer = 0
    x2d = x.reshape(n, d)
    proj = _inproj(x2d, norm_mix_g[layer][None], w_in[layer].astype(BF16), tm=512)
    a_out = _attention(proj.reshape(b, s, IN_WIDTH))
    sgb_full = jnp.repeat(jnp.transpose(sg_b[layer]), SG_GROUP_DIM, axis=1)
    out = _mix_mlp(
        x2d, a_out.reshape(n, SB_WIDTH), proj,
        sg_ln_g[layer][None], sg_ln_b[layer][None], sg_w[layer], sgb_full,
        out_norm_g[layer][None], w_out[layer].astype(BF16), norm_mlp_g[layer][None],
        w_up[layer].astype(BF16), w_down[layer].astype(BF16), norm_final_g[None], tm=512)
    return out.reshape(b, s, d)
```

```python
import jax
import jax.numpy as jnp
from jax import lax
from jax.experimental import pallas as pl
from jax.experimental.pallas import tpu as pltpu

D_MODEL = 1024
CHUNK = 64
SB_HEADS = 8
SB_HEAD_DIM = 64
SB_WIDTH = SB_HEADS * SB_HEAD_DIM
SG_GROUPS = 8
SG_GROUP_DIM = 64
SG_WIDTH = SG_GROUPS * SG_GROUP_DIM
SG_BLOCK = 128
MIX_WIDTH = SB_WIDTH + SG_WIDTH
IN_WIDTH = 3 * SB_WIDTH + 2 * SG_WIDTH
D_FF = 4 * D_MODEL
EPS = 1e-6

LANES = 128
ATT_BLOCK = 128
SUPER = 2 * ATT_BLOCK
ITEM_ROWS = 2 * SUPER
HEAD_PAIRS = SB_HEADS * SB_HEAD_DIM // LANES
Q_SCALE = 1.4426950408889634 / (SB_HEAD_DIM ** 0.5)
VMEM_LIMIT_BYTES = 56 * 1024 * 1024

F32 = jnp.float32
BF16 = jnp.bfloat16


def _rms(x, g):
    ms = jnp.mean(x * x, axis=-1, keepdims=True)
    return x * lax.rsqrt(ms + EPS) * g


def _inproj_kernel(x_ref, g_ref, w_ref, o_ref):
    h = _rms(x_ref[...], g_ref[...]).astype(BF16)
    proj = jnp.dot(h, w_ref[...], preferred_element_type=F32)
    col = lax.broadcasted_iota(jnp.int32, (1, IN_WIDTH), 1)
    colscale = jnp.where(col < SB_WIDTH, Q_SCALE, 1.0).astype(F32)
    o_ref[...] = (proj * colscale).astype(o_ref.dtype)


def _inproj(x2d, g, w_bf16, tm):
    n = x2d.shape[0]
    return pl.pallas_call(
        _inproj_kernel,
        grid=(n // tm,),
        in_specs=[
            pl.BlockSpec((tm, D_MODEL), lambda i: (i, 0)),
            pl.BlockSpec((1, D_MODEL), lambda i: (0, 0)),
            pl.BlockSpec((D_MODEL, IN_WIDTH), lambda i: (0, 0), pipeline_mode=pl.Buffered(1)),
        ],
        out_specs=pl.BlockSpec((tm, IN_WIDTH), lambda i: (i, 0)),
        out_shape=jax.ShapeDtypeStruct((n, IN_WIDTH), BF16),
        compiler_params=pltpu.CompilerParams(
            dimension_semantics=("arbitrary",), vmem_limit_bytes=VMEM_LIMIT_BYTES),
        name="inproj",
    )(x2d, g, w_bf16)


MASKED = -1e30
ITEMS_PER_STAGE = 1
TRIP_UNROLL = 4
ROW_CHUNK = 32


def _attn_kernel(q_ref, k_ref, v_ref, o_ref, qm_ref, bias_ref, u_ref, xs_ref, z_ref, a_ref,
                 carry_ref, acc_ref):
    s_len = q_ref.shape[1]
    nsup = s_len // SUPER
    blk = ATT_BLOCK

    @pl.when((pl.program_id(0) == 0) & (pl.program_id(1) == 0))
    def _():
        drow = lax.broadcasted_iota(jnp.int32, (ITEM_ROWS, SUPER), 0)
        dcol = lax.broadcasted_iota(jnp.int32, (ITEM_ROWS, SUPER), 1)
        qpos = ((drow >> 8) << 7) + (drow & (blk - 1))
        bias_ref[0] = jnp.zeros((ITEM_ROWS, SUPER), F32)
        bias_ref[1] = jnp.where(dcol < qpos, 0.0, MASKED).astype(F32)
        krow = lax.broadcasted_iota(jnp.int32, (2 * blk, 2 * blk), 0) & (blk - 1)
        kcol = lax.broadcasted_iota(jnp.int32, (2 * blk, 2 * blk), 1)
        u_ref[...] = jnp.where((krow >= kcol) | (kcol >= blk), -1.0, 0.0).astype(BF16)

    lane = lax.broadcasted_iota(jnp.int32, (blk, LANES), 1)
    first_head = lane < SB_HEAD_DIM
    for m in range(nsup):
        for half in range(2):
            qb = q_ref[0, (2 * m + half) * blk:(2 * m + half + 1) * blk, :]
            zero = jnp.zeros_like(qb)
            qm_ref[m, (2 * half) * blk:(2 * half + 1) * blk, :] = jnp.where(first_head, qb, zero)
            qm_ref[m, (2 * half + 1) * blk:(2 * half + 2) * blk, :] = jnp.where(first_head, zero, qb)
    carry_ref[...] = jnp.zeros_like(carry_ref)
    acc_ref[...] = jnp.zeros_like(acc_ref)

    def key_rows(c):
        off = c * SUPER
        return pl.ds(off if isinstance(off, int) else pl.multiple_of(off, SUPER), SUPER)

    def scores(item, slot):
        m, c = item
        diag = int(m == c) if isinstance(m, int) else jnp.where(m == c, 1, 0)
        zfull = lax.dot_general(qm_ref[m], k_ref[0, key_rows(c), :], (((1,), (1,)), ((), ())),
                                preferred_element_type=F32)
        for rc in range(ITEM_ROWS // ROW_CHUNK):
            rows = slice(rc * ROW_CHUNK, (rc + 1) * ROW_CHUNK)
            z2 = zfull[rows] + bias_ref[diag, rows, :]
            sp = jnp.maximum(z2, 0.0) + jnp.log2(1.0 + jnp.exp2(-jnp.abs(z2)))
            z_ref[slot, rows, :] = z2
            for jb in range(2):
                spb = sp[:, jb * blk:(jb + 1) * blk]
                hi = spb.astype(BF16)
                lo = (spb - hi.astype(F32)).astype(BF16)
                x0 = (2 * rc + 1 - jb) * ROW_CHUNK
                xs_ref[slot, x0:x0 + ROW_CHUNK, :] = jnp.concatenate([hi, lo], axis=1)

    def weights(item, slot_in, slot_out):
        m, _ = item
        st = jnp.dot(xs_ref[slot_in], u_ref[...], preferred_element_type=F32)
        for rc in range(ITEM_ROWS // ROW_CHUNK):
            rows = slice(rc * ROW_CHUNK, (rc + 1) * ROW_CHUNK)
            carry = carry_ref[m, rows, :]
            a_blocks = [None, None]
            for jb in (1, 0):
                x0 = (2 * rc + 1 - jb) * ROW_CHUNK
                st_b = st[x0:x0 + ROW_CHUNK]
                z2 = z_ref[slot_in, rows, jb * blk:(jb + 1) * blk]
                a_blocks[jb] = jnp.exp2(z2 + (st_b[:, :blk] + carry)).astype(BF16)
                carry = carry + st_b[:, blk:]
            carry_ref[m, rows, :] = carry
            a_ref[slot_out, rows, :] = jnp.concatenate(a_blocks, axis=1)

    def values(item, slot):
        m, c = item
        acc_ref[m] += jnp.dot(a_ref[slot], v_ref[0, key_rows(c), :], preferred_element_type=F32)

    def advance(item):
        m, c = item
        last = c == 0
        return jnp.where(last, m + 1, m), jnp.where(last, m + 1, c - 1)

    def trip(cur, s3_items, s2_items, s1_items):
        prev = 1 - cur
        n = ITEMS_PER_STAGE
        for i, item in enumerate(s1_items):
            scores(item, cur * n + i)
        for i, item in enumerate(s3_items):
            values(item, prev * n + i)
        for i, item in enumerate(s2_items):
            weights(item, prev * n + i, cur * n + i)

    items = [(m, c) for m in range(nsup) for c in range(m, -1, -1)]
    n_trips = len(items) // ITEMS_PER_STAGE
    assert len(items) % ITEMS_PER_STAGE == 0 and n_trips % 2 == 0
    pair = lambda t: items[ITEMS_PER_STAGE * t:ITEMS_PER_STAGE * (t + 1)] if 0 <= t < n_trips else []

    lead = 2 + (n_trips - 2) % TRIP_UNROLL
    assert lead % 2 == 0 and TRIP_UNROLL % 2 == 0 and n_trips > lead
    for t in range(lead):
        trip(t & 1, pair(t - 2), pair(t - 1), pair(t))

    def steady(_, state):
        s3_items, s2_items, nxt = state
        for u in range(TRIP_UNROLL):
            s1_items = []
            for _ in range(ITEMS_PER_STAGE):
                s1_items.append(nxt)
                nxt = advance(nxt)
            trip(u & 1, s3_items, s2_items, s1_items)
            s3_items, s2_items = s2_items, tuple(s1_items)
        return s3_items, s2_items, nxt

    as_i32 = lambda its: tuple((jnp.int32(m), jnp.int32(c)) for m, c in its)
    lax.fori_loop(0, (n_trips - lead) // TRIP_UNROLL, steady,
                  (as_i32(pair(lead - 2)), as_i32(pair(lead - 1)), as_i32(pair(lead)[:1])[0]))
    trip(0, pair(n_trips - 2), pair(n_trips - 1), [])
    trip(1, pair(n_trips - 1), [], [])

    for m in range(nsup):
        for half in range(2):
            h0 = acc_ref[m, (2 * half) * blk:(2 * half + 1) * blk, :]
            h1 = acc_ref[m, (2 * half + 1) * blk:(2 * half + 2) * blk, :]
            o_ref[0, (2 * m + half) * blk:(2 * m + half + 1) * blk, :] = jnp.where(first_head, h0, h1)


def _attention(proj3d):
    b, s, _ = proj3d.shape
    nsup = s // SUPER
    nslot = 2 * ITEMS_PER_STAGE
    spec = lambda off: pl.BlockSpec((1, s, LANES), lambda bi, p: (bi, 0, off + p))
    return pl.pallas_call(
        _attn_kernel,
        grid=(b, HEAD_PAIRS),
        in_specs=[spec(0), spec(HEAD_PAIRS), spec(2 * HEAD_PAIRS)],
        out_specs=pl.BlockSpec((1, s, LANES), lambda bi, p: (bi, 0, p)),
        out_shape=jax.ShapeDtypeStruct((b, s, SB_WIDTH), F32),
        scratch_shapes=[
            pltpu.VMEM((nsup, ITEM_ROWS, LANES), BF16),
            pltpu.VMEM((2, ITEM_ROWS, SUPER), F32),
            pltpu.VMEM((2 * ATT_BLOCK, 2 * ATT_BLOCK), BF16),
            pltpu.VMEM((nslot, 2 * ITEM_ROWS, 2 * ATT_BLOCK), BF16),
            pltpu.VMEM((nslot, ITEM_ROWS, SUPER), F32),
            pltpu.VMEM((nslot, ITEM_ROWS, SUPER), BF16),
            pltpu.VMEM((nsup, ITEM_ROWS, ATT_BLOCK), F32),
            pltpu.VMEM((nsup, ITEM_ROWS, LANES), F32),
        ],
        compiler_params=pltpu.CompilerParams(
            dimension_semantics=("arbitrary", "arbitrary"), vmem_limit_bytes=VMEM_LIMIT_BYTES),
        name="stickbreak_attn",
    )(proj3d, proj3d, proj3d)


def _mix_mlp_kernel(x_ref, a_ref, u_ref, vg_ref, lng_ref, lnb_ref, sgw_ref, sgb_ref,
                    ong_ref, wout_ref, nmg_ref, wup_ref, wdn_ref, nfg_ref, o_ref,
                    mixed_ref):
    tm = x_ref.shape[0]
    vact = jax.nn.gelu(vg_ref[...].astype(F32))
    mu = jnp.mean(vact, axis=-1, keepdims=True)
    vc = vact - mu
    var = jnp.mean(vc * vc, axis=-1, keepdims=True)
    vn = (vc * lax.rsqrt(var + EPS) * lng_ref[...] + lnb_ref[...]).astype(BF16)

    pos_t = lax.broadcasted_iota(jnp.int32, (SG_BLOCK, SG_BLOCK), 0)
    pos_s = lax.broadcasted_iota(jnp.int32, (SG_BLOCK, SG_BLOCK), 1)
    chunk_mask = (pos_s // CHUNK) <= (pos_t // CHUNK)
    lane = lax.broadcasted_iota(jnp.int32, (SG_BLOCK, LANES), 1)
    first = lane < SG_GROUP_DIM
    zero = jnp.zeros((SG_BLOCK, LANES), BF16)
    for pair in range(SG_GROUPS // 2):
        w0 = jnp.where(chunk_mask, sgw_ref[2 * pair], 0.0).astype(BF16)
        w1 = jnp.where(chunk_mask, sgw_ref[2 * pair + 1], 0.0).astype(BF16)
        w_pair = jnp.concatenate([w0, w1], axis=1)
        for r in range(tm // SG_BLOCK):
            vblk = vn[r * SG_BLOCK:(r + 1) * SG_BLOCK, pair * LANES:(pair + 1) * LANES]
            rhs = jnp.concatenate([jnp.where(first, vblk, zero), jnp.where(first, zero, vblk)], axis=0)
            mixed_ref[r * SG_BLOCK:(r + 1) * SG_BLOCK, pair * LANES:(pair + 1) * LANES] = (
                jnp.dot(w_pair, rhs, preferred_element_type=F32)
                + sgb_ref[:, pair * LANES:(pair + 1) * LANES])
    b_out = jax.nn.gelu(u_ref[...].astype(F32)) * mixed_ref[...]

    ong = ong_ref[...]
    a_n = _rms(a_ref[...], ong[:, :SB_WIDTH]).astype(BF16)
    b_n = _rms(b_out, ong[:, SB_WIDTH:]).astype(BF16)
    mix = jnp.concatenate([a_n, b_n], axis=1)
    x1 = x_ref[...] + jnp.dot(mix, wout_ref[...], preferred_element_type=F32)

    h = _rms(x1, nmg_ref[...]).astype(BF16)
    y = x1
    fc = D_MODEL
    for c in range(D_FF // fc):
        up = jnp.dot(h, wup_ref[:, c * fc:(c + 1) * fc], preferred_element_type=F32)
        act = jnp.square(jnp.maximum(up, 0.0)).astype(BF16)
        y = y + jnp.dot(act, wdn_ref[c * fc:(c + 1) * fc, :], preferred_element_type=F32)
    o_ref[...] = _rms(y, nfg_ref[...])


def _mix_mlp(x2d, a2d, proj2d, lng, lnb, sgw, sgb_full, ong, wout, nmg, wup, wdn, nfg, tm):
    n = x2d.shape[0]
    const = lambda shape: pl.BlockSpec(shape, lambda i: (0,) * len(shape),
                                       pipeline_mode=pl.Buffered(1))
    u_blk = 3 * SB_WIDTH // SG_WIDTH
    return pl.pallas_call(
        _mix_mlp_kernel,
        grid=(n // tm,),
        in_specs=[
            pl.BlockSpec((tm, D_MODEL), lambda i: (i, 0)),
            pl.BlockSpec((tm, SB_WIDTH), lambda i: (i, 0)),
            pl.BlockSpec((tm, SG_WIDTH), lambda i: (i, u_blk)),
            pl.BlockSpec((tm, SG_WIDTH), lambda i: (i, u_blk + 1)),
            const((1, SG_WIDTH)), const((1, SG_WIDTH)),
            const((SG_GROUPS, SG_BLOCK, SG_BLOCK)), const((SG_BLOCK, SG_WIDTH)),
            const((1, MIX_WIDTH)), const((MIX_WIDTH, D_MODEL)),
            const((1, D_MODEL)), const((D_MODEL, D_FF)), const((D_FF, D_MODEL)),
            const((1, D_MODEL)),
        ],
        out_specs=pl.BlockSpec((tm, D_MODEL), lambda i: (i, 0)),
        out_shape=jax.ShapeDtypeStruct((n, D_MODEL), F32),
        scratch_shapes=[pltpu.VMEM((tm, SG_WIDTH), F32)],
        compiler_params=pltpu.CompilerParams(
            dimension_semantics=("arbitrary",), vmem_limit_bytes=VMEM_LIMIT_BYTES),
        name="mix_mlp",
    )(x2d, a2d, proj2d, proj2d, lng, lnb, sgw, sgb_full, ong, wout, nmg, wup, wdn, nfg)


def kernel(x, norm_mix_g, w_in, sg_ln_g, sg_ln_b, sg_w, sg_b, out_norm_g, w_out,
           norm_mlp_g, w_up, w_down, norm_final_g):
    b, s, d = x.shape
    n = b * s
    assert w_in.shape[0] == 1, "the final norm is fused into the (single) layer's last kernel"
    layer = 0
    x2d = x.reshape(n, d)
    proj = _inproj(x2d, norm_mix_g[layer][None], w_in[layer].astype(BF16), tm=512)
    a_out = _attention(proj.reshape(b, s, IN_WIDTH))
    sgb_full = jnp.repeat(jnp.transpose(sg_b[layer]), SG_GROUP_DIM, axis=1)
    out = _mix_mlp(
        x2d, a_out.reshape(n, SB_WIDTH), proj,
        sg_ln_g[layer][None], sg_ln_b[layer][None], sg_w[layer], sgb_full,
        out_norm_g[layer][None], w_out[layer].astype(BF16), norm_mlp_g[layer][None],
        w_up[layer].astype(BF16), w_down[layer].astype(BF16), norm_final_g[None], tm=512)
    return out.reshape(b, s, d)
```

```python
import jax
import jax.numpy as jnp
from jax import lax
from jax.experimental import pallas as pl
from jax.experimental.pallas import tpu as pltpu

D_MODEL = 1024
CHUNK = 64
SB_HEADS = 8
SB_HEAD_DIM = 64
SB_WIDTH = SB_HEADS * SB_HEAD_DIM
SG_GROUPS = 8
SG_GROUP_DIM = 64
SG_WIDTH = SG_GROUPS * SG_GROUP_DIM
SG_BLOCK = 128
MIX_WIDTH = SB_WIDTH + SG_WIDTH
IN_WIDTH = 3 * SB_WIDTH + 2 * SG_WIDTH
D_FF = 4 * D_MODEL
EPS = 1e-6

LANES = 128
ATT_BLOCK = 128
SUPER = 2 * ATT_BLOCK
ITEM_ROWS = 2 * SUPER
HEAD_PAIRS = SB_HEADS * SB_HEAD_DIM // LANES
Q_SCALE = 1.4426950408889634 / (SB_HEAD_DIM ** 0.5)
VMEM_LIMIT_BYTES = 56 * 1024 * 1024

F32 = jnp.float32
BF16 = jnp.bfloat16


def _rms(x, g):
    ms = jnp.mean(x * x, axis=-1, keepdims=True)
    return x * lax.rsqrt(ms + EPS) * g


def _inproj_kernel(x_ref, g_ref, w_ref, lng_ref, lnb_ref, o_ref):
    qkv = 3 * SB_WIDTH
    half = SG_WIDTH // 2
    h = _rms(x_ref[...], g_ref[...]).astype(BF16)

    def cols(lo, width):
        return jnp.dot(h, w_ref[:, lo:lo + width], preferred_element_type=F32)

    vacts = []
    for step, part in enumerate((2, 3, 0, 1)):
        lo = qkv + part * half
        act = jax.nn.gelu(cols(lo, half))
        if part < 2:
            o_ref[:, lo:lo + half] = act.astype(o_ref.dtype)
        else:
            vacts.append(act)
        if part == 3:
            vact = jnp.concatenate(vacts, axis=1)
            mu = jnp.mean(vact, axis=-1, keepdims=True)
            vc = vact - mu
            var = jnp.mean(vc * vc, axis=-1, keepdims=True)
            vn = vc * lax.rsqrt(var + EPS) * lng_ref[...] + lnb_ref[...]
            o_ref[:, qkv + SG_WIDTH:] = vn.astype(o_ref.dtype)
        if step < 3:
            p = cols(step * SB_WIDTH, SB_WIDTH)
            if step == 0:
                p = p * Q_SCALE
            o_ref[:, step * SB_WIDTH:(step + 1) * SB_WIDTH] = p.astype(o_ref.dtype)


def _inproj(x2d, g, w_bf16, lng, lnb, tm):
    n = x2d.shape[0]
    const = lambda shape: pl.BlockSpec(shape, lambda i: (0, 0), pipeline_mode=pl.Buffered(1))
    return pl.pallas_call(
        _inproj_kernel,
        grid=(n // tm,),
        in_specs=[
            pl.BlockSpec((tm, D_MODEL), lambda i: (i, 0)),
            const((1, D_MODEL)), const((D_MODEL, IN_WIDTH)),
            const((1, SG_WIDTH)), const((1, SG_WIDTH)),
        ],
        out_specs=pl.BlockSpec((tm, IN_WIDTH), lambda i: (i, 0)),
        out_shape=jax.ShapeDtypeStruct((n, IN_WIDTH), BF16),
        compiler_params=pltpu.CompilerParams(
            dimension_semantics=("arbitrary",), vmem_limit_bytes=VMEM_LIMIT_BYTES),
        name="inproj",
    )(x2d, g, w_bf16, lng, lnb)


MASKED = -1e30
ITEMS_PER_STAGE = 1
TRIP_UNROLL = 8
ROW_CHUNK = 32


def _attn_kernel(q_ref, k_ref, v_ref, o_ref, qm_ref, bias_ref, u_ref, xs_ref, z_ref, a_ref,
                 carry_ref, acc_ref):
    s_len = q_ref.shape[1]
    nsup = s_len // SUPER
    blk = ATT_BLOCK

    @pl.when((pl.program_id(0) == 0) & (pl.program_id(1) == 0))
    def _():
        drow = lax.broadcasted_iota(jnp.int32, (ITEM_ROWS, SUPER), 0)
        dcol = lax.broadcasted_iota(jnp.int32, (ITEM_ROWS, SUPER), 1)
        qpos = ((drow >> 8) << 7) + (drow & (blk - 1))
        bias_ref[0] = jnp.zeros((ITEM_ROWS, SUPER), F32)
        bias_ref[1] = jnp.where(dcol < qpos, 0.0, MASKED).astype(F32)
        krow = lax.broadcasted_iota(jnp.int32, (2 * blk, 2 * blk), 0) & (blk - 1)
        kcol = lax.broadcasted_iota(jnp.int32, (2 * blk, 2 * blk), 1)
        u_ref[...] = jnp.where((krow >= kcol) | (kcol >= blk), -1.0, 0.0).astype(BF16)

    lane = lax.broadcasted_iota(jnp.int32, (blk, LANES), 1)
    first_head = lane < SB_HEAD_DIM
    for m in range(nsup):
        for half in range(2):
            qb = q_ref[0, (2 * m + half) * blk:(2 * m + half + 1) * blk, :]
            zero = jnp.zeros_like(qb)
            qm_ref[m, (2 * half) * blk:(2 * half + 1) * blk, :] = jnp.where(first_head, qb, zero)
            qm_ref[m, (2 * half + 1) * blk:(2 * half + 2) * blk, :] = jnp.where(first_head, zero, qb)
    carry_ref[...] = jnp.zeros_like(carry_ref)
    acc_ref[...] = jnp.zeros_like(acc_ref)

    def key_rows(c):
        off = c * SUPER
        return pl.ds(off if isinstance(off, int) else pl.multiple_of(off, SUPER), SUPER)

    def scores(item, slot):
        m, c = item
        diag = int(m == c) if isinstance(m, int) else jnp.where(m == c, 1, 0)
        zfull = lax.dot_general(qm_ref[m], k_ref[0, key_rows(c), :], (((1,), (1,)), ((), ())),
                                preferred_element_type=F32)
        for rc in range(ITEM_ROWS // ROW_CHUNK):
            rows = slice(rc * ROW_CHUNK, (rc + 1) * ROW_CHUNK)
            z2 = zfull[rows] + bias_ref[diag, rows, :]
            sp = jnp.maximum(z2, 0.0) + jnp.log2(1.0 + jnp.exp2(-jnp.abs(z2)))
            z_ref[slot, rows, :] = z2
            for jb in range(2):
                spb = sp[:, jb * blk:(jb + 1) * blk]
                hi = spb.astype(BF16)
                lo = (spb - hi.astype(F32)).astype(BF16)
                x0 = (2 * rc + 1 - jb) * ROW_CHUNK
                xs_ref[slot, x0:x0 + ROW_CHUNK, :] = jnp.concatenate([hi, lo], axis=1)

    def weights(item, slot_in, slot_out):
        m, _ = item
        st = jnp.dot(xs_ref[slot_in], u_ref[...], preferred_element_type=F32)
        for rc in range(ITEM_ROWS // ROW_CHUNK):
            rows = slice(rc * ROW_CHUNK, (rc + 1) * ROW_CHUNK)
            carry = carry_ref[m, rows, :]
            a_blocks = [None, None]
            for jb in (1, 0):
                x0 = (2 * rc + 1 - jb) * ROW_CHUNK
                st_b = st[x0:x0 + ROW_CHUNK]
                z2 = z_ref[slot_in, rows, jb * blk:(jb + 1) * blk]
                a_blocks[jb] = jnp.exp2(z2 + (st_b[:, :blk] + carry)).astype(BF16)
                carry = carry + st_b[:, blk:]
            carry_ref[m, rows, :] = carry
            a_ref[slot_out, rows, :] = jnp.concatenate(a_blocks, axis=1)

    def values(item, slot):
        m, c = item
        acc_ref[m] += jnp.dot(a_ref[slot], v_ref[0, key_rows(c), :], preferred_element_type=F32)

    def advance(item):
        m, c = item
        last = c == 0
        return jnp.where(last, m + 1, m), jnp.where(last, m + 1, c - 1)

    def trip(cur, s3_items, s2_items, s1_items):
        prev = 1 - cur
        n = ITEMS_PER_STAGE
        for i, item in enumerate(s1_items):
            scores(item, cur * n + i)
        for i, item in enumerate(s3_items):
            values(item, prev * n + i)
        for i, item in enumerate(s2_items):
            weights(item, prev * n + i, cur * n + i)

    items = [(m, c) for m in range(nsup) for c in range(m, -1, -1)]
    n_trips = len(items) // ITEMS_PER_STAGE
    assert len(items) % ITEMS_PER_STAGE == 0 and n_trips % 2 == 0
    pair = lambda t: items[ITEMS_PER_STAGE * t:ITEMS_PER_STAGE * (t + 1)] if 0 <= t < n_trips else []

    lead = 2 + (n_trips - 2) % TRIP_UNROLL
    assert lead % 2 == 0 and TRIP_UNROLL % 2 == 0 and n_trips > lead
    for t in range(lead):
        trip(t & 1, pair(t - 2), pair(t - 1), pair(t))

    def steady(_, state):
        s3_items, s2_items, nxt = state
        for u in range(TRIP_UNROLL):
            s1_items = []
            for _ in range(ITEMS_PER_STAGE):
                s1_items.append(nxt)
                nxt = advance(nxt)
            trip(u & 1, s3_items, s2_items, s1_items)
            s3_items, s2_items = s2_items, tuple(s1_items)
        return s3_items, s2_items, nxt

    as_i32 = lambda its: tuple((jnp.int32(m), jnp.int32(c)) for m, c in its)
    lax.fori_loop(0, (n_trips - lead) // TRIP_UNROLL, steady,
                  (as_i32(pair(lead - 2)), as_i32(pair(lead - 1)), as_i32(pair(lead)[:1])[0]))
    trip(0, pair(n_trips - 2), pair(n_trips - 1), [])
    trip(1, pair(n_trips - 1), [], [])

    for m in range(nsup):
        for half in range(2):
            h0 = acc_ref[m, (2 * half) * blk:(2 * half + 1) * blk, :]
            h1 = acc_ref[m, (2 * half + 1) * blk:(2 * half + 2) * blk, :]
            o_ref[0, (2 * m + half) * blk:(2 * m + half + 1) * blk, :] = jnp.where(first_head, h0, h1)


def _attention(proj3d):
    b, s, _ = proj3d.shape
    nsup = s // SUPER
    nslot = 2 * ITEMS_PER_STAGE
    spec = lambda off: pl.BlockSpec((1, s, LANES), lambda bi, p: (bi, 0, off + p))
    return pl.pallas_call(
        _attn_kernel,
        grid=(b, HEAD_PAIRS),
        in_specs=[spec(0), spec(HEAD_PAIRS), spec(2 * HEAD_PAIRS)],
        out_specs=pl.BlockSpec((1, s, LANES), lambda bi, p: (bi, 0, p)),
        out_shape=jax.ShapeDtypeStruct((b, s, SB_WIDTH), F32),
        scratch_shapes=[
            pltpu.VMEM((nsup, ITEM_ROWS, LANES), BF16),
            pltpu.VMEM((2, ITEM_ROWS, SUPER), F32),
            pltpu.VMEM((2 * ATT_BLOCK, 2 * ATT_BLOCK), BF16),
            pltpu.VMEM((nslot, 2 * ITEM_ROWS, 2 * ATT_BLOCK), BF16),
            pltpu.VMEM((nslot, ITEM_ROWS, SUPER), F32),
            pltpu.VMEM((nslot, ITEM_ROWS, SUPER), BF16),
            pltpu.VMEM((nsup, ITEM_ROWS, ATT_BLOCK), F32),
            pltpu.VMEM((nsup, ITEM_ROWS, LANES), F32),
        ],
        compiler_params=pltpu.CompilerParams(
            dimension_semantics=("arbitrary", "arbitrary"), vmem_limit_bytes=VMEM_LIMIT_BYTES),
        name="stickbreak_attn",
    )(proj3d, proj3d, proj3d)


def _mix_mlp_kernel(x_ref, a_ref, u_ref, vn_ref, sgw_ref, sgb_ref,
                    ong_ref, wout_ref, nmg_ref, wup_ref, wdn_ref, nfg_ref, o_ref,
                    mixed_ref):
    tm = x_ref.shape[0]
    vn = vn_ref[...]

    pos_t = lax.broadcasted_iota(jnp.int32, (SG_BLOCK, SG_BLOCK), 0)
    pos_s = lax.broadcasted_iota(jnp.int32, (SG_BLOCK, SG_BLOCK), 1)
    chunk_mask = (pos_s // CHUNK) <= (pos_t // CHUNK)
    lane = lax.broadcasted_iota(jnp.int32, (SG_BLOCK, LANES), 1)
    first = lane < SG_GROUP_DIM
    zero = jnp.zeros((SG_BLOCK, LANES), BF16)
    for pair in range(SG_GROUPS // 2):
        w0 = jnp.where(chunk_mask, sgw_ref[2 * pair], 0.0).astype(BF16)
        w1 = jnp.where(chunk_mask, sgw_ref[2 * pair + 1], 0.0).astype(BF16)
        w_pair = jnp.concatenate([w0, w1], axis=1)
        for r in range(tm // SG_BLOCK):
            vblk = vn[r * SG_BLOCK:(r + 1) * SG_BLOCK, pair * LANES:(pair + 1) * LANES]
            rhs = jnp.concatenate([jnp.where(first, vblk, zero), jnp.where(first, zero, vblk)], axis=0)
            mixed_ref[r * SG_BLOCK:(r + 1) * SG_BLOCK, pair * LANES:(pair + 1) * LANES] = (
                jnp.dot(w_pair, rhs, preferred_element_type=F32)
                + sgb_ref[:, pair * LANES:(pair + 1) * LANES])
    b_out = u_ref[...].astype(F32) * mixed_ref[...]

    ong = ong_ref[...]
    a_n = _rms(a_ref[...], ong[:, :SB_WIDTH]).astype(BF16)
    b_n = _rms(b_out, ong[:, SB_WIDTH:]).astype(BF16)
    mix = jnp.concatenate([a_n, b_n], axis=1)
    x1 = x_ref[...] + jnp.dot(mix, wout_ref[...], preferred_element_type=F32)

    h = _rms(x1, nmg_ref[...]).astype(BF16)
    y = x1
    fc = D_MODEL
    for c in range(D_FF // fc):
        up = jnp.dot(h, wup_ref[:, c * fc:(c + 1) * fc], preferred_element_type=F32)
        act = jnp.square(jnp.maximum(up, 0.0)).astype(BF16)
        y = y + jnp.dot(act, wdn_ref[c * fc:(c + 1) * fc, :], preferred_element_type=F32)
    o_ref[...] = _rms(y, nfg_ref[...])


def _mix_mlp(x2d, a2d, proj2d, sgw, sgb_full, ong, wout, nmg, wup, wdn, nfg, tm):
    n = x2d.shape[0]
    const = lambda shape: pl.BlockSpec(shape, lambda i: (0,) * len(shape),
                                       pipeline_mode=pl.Buffered(1))
    u_blk = 3 * SB_WIDTH // SG_WIDTH
    return pl.pallas_call(
        _mix_mlp_kernel,
        grid=(n // tm,),
        in_specs=[
            pl.BlockSpec((tm, D_MODEL), lambda i: (i, 0)),
            pl.BlockSpec((tm, SB_WIDTH), lambda i: (i, 0)),
            pl.BlockSpec((tm, SG_WIDTH), lambda i: (i, u_blk)),
            pl.BlockSpec((tm, SG_WIDTH), lambda i: (i, u_blk + 1)),
            const((SG_GROUPS, SG_BLOCK, SG_BLOCK)), const((SG_BLOCK, SG_WIDTH)),
            const((1, MIX_WIDTH)), const((MIX_WIDTH, D_MODEL)),
            const((1, D_MODEL)), const((D_MODEL, D_FF)), const((D_FF, D_MODEL)),
            const((1, D_MODEL)),
        ],
        out_specs=pl.BlockSpec((tm, D_MODEL), lambda i: (i, 0)),
        out_shape=jax.ShapeDtypeStruct((n, D_MODEL), F32),
        scratch_shapes=[pltpu.VMEM((tm, SG_WIDTH), F32)],
        compiler_params=pltpu.CompilerParams(
            dimension_semantics=("arbitrary",), vmem_limit_bytes=VMEM_LIMIT_BYTES),
        name="mix_mlp",
    )(x2d, a2d, proj2d, proj2d, sgw, sgb_full, ong, wout, nmg, wup, wdn, nfg)


def kernel(x, norm_mix_g, w_in, sg_ln_g, sg_ln_b, sg_w, sg_b, out_norm_g, w_out,
           norm_mlp_g, w_up, w_down, norm_final_g):
    b, s, d = x.shape
    n = b * s
    assert w_in.shape[0] == 1, "the final norm is fused into the (single) layer's last kernel"
    layer = 0
    x2d = x.reshape(n, d)
    proj = _inproj(x2d, norm_mix_g[layer][None], w_in[layer].astype(BF16),
                   sg_ln_g[layer][None], sg_ln_b[layer][None], tm=512)
    a_out = _attention(proj.reshape(b, s, IN_WIDTH))
    sgb_full = jnp.repeat(jnp.transpose(sg_b[layer]), SG_GROUP_DIM, axis=1)
    out = _mix_mlp(
        x2d, a_out.reshape(n, SB_WIDTH), proj, sg_w[layer], sgb_full,
        out_norm_g[layer][None], w_out[layer].astype(BF16), norm_mlp_g[layer][None],
        w_up[layer].astype(BF16), w_down[layer].astype(BF16), norm_final_g[None], tm=512)
    return out.reshape(b, s, d)
```

```python
import jax
import jax.numpy as jnp
from jax import lax
from jax.experimental import pallas as pl
from jax.experimental.pallas import tpu as pltpu

D_MODEL = 1024
CHUNK = 64
SB_HEADS = 8
SB_HEAD_DIM = 64
SB_WIDTH = SB_HEADS * SB_HEAD_DIM
SG_GROUPS = 8
SG_GROUP_DIM = 64
SG_WIDTH = SG_GROUPS * SG_GROUP_DIM
SG_BLOCK = 128
MIX_WIDTH = SB_WIDTH + SG_WIDTH
IN_WIDTH = 3 * SB_WIDTH + 2 * SG_WIDTH
D_FF = 4 * D_MODEL
EPS = 1e-6

LANES = 128
ATT_BLOCK = 128
SUPER = 2 * ATT_BLOCK
ITEM_ROWS = 2 * SUPER
HEAD_PAIRS = SB_HEADS * SB_HEAD_DIM // LANES
Q_SCALE = 1.4426950408889634 / (SB_HEAD_DIM ** 0.5)
VMEM_LIMIT_BYTES = 56 * 1024 * 1024

F32 = jnp.float32
BF16 = jnp.bfloat16


def _rms(x, g):
    ms = jnp.mean(x * x, axis=-1, keepdims=True)
    return x * lax.rsqrt(ms + EPS) * g


def _inproj_kernel(x_ref, g_ref, w_ref, lng_ref, lnb_ref, o_ref):
    qkv = 3 * SB_WIDTH
    half = SG_WIDTH // 2
    h = _rms(x_ref[...], g_ref[...]).astype(BF16)

    def cols(lo, width):
        return jnp.dot(h, w_ref[:, lo:lo + width], preferred_element_type=F32)

    vacts = []
    for step, part in enumerate((2, 3, 0, 1)):
        lo = qkv + part * half
        act = jax.nn.gelu(cols(lo, half))
        if part < 2:
            o_ref[:, lo:lo + half] = act.astype(o_ref.dtype)
        else:
            vacts.append(act)
        if part == 3:
            vact = jnp.concatenate(vacts, axis=1)
            mu = jnp.mean(vact, axis=-1, keepdims=True)
            vc = vact - mu
            var = jnp.mean(vc * vc, axis=-1, keepdims=True)
            vn = vc * lax.rsqrt(var + EPS) * lng_ref[...] + lnb_ref[...]
            o_ref[:, qkv + SG_WIDTH:] = vn.astype(o_ref.dtype)
        if step < 3:
            p = cols(step * SB_WIDTH, SB_WIDTH)
            if step == 0:
                p = p * Q_SCALE
            o_ref[:, step * SB_WIDTH:(step + 1) * SB_WIDTH] = p.astype(o_ref.dtype)


def _inproj(x2d, g, w_bf16, lng, lnb, tm):
    n = x2d.shape[0]
    const = lambda shape: pl.BlockSpec(shape, lambda i: (0, 0), pipeline_mode=pl.Buffered(1))
    return pl.pallas_call(
        _inproj_kernel,
        grid=(n // tm,),
        in_specs=[
            pl.BlockSpec((tm, D_MODEL), lambda i: (i, 0)),
            const((1, D_MODEL)), const((D_MODEL, IN_WIDTH)),
            const((1, SG_WIDTH)), const((1, SG_WIDTH)),
        ],
        out_specs=pl.BlockSpec((tm, IN_WIDTH), lambda i: (i, 0)),
        out_shape=jax.ShapeDtypeStruct((n, IN_WIDTH), BF16),
        compiler_params=pltpu.CompilerParams(
            dimension_semantics=("arbitrary",), vmem_limit_bytes=VMEM_LIMIT_BYTES),
        name="inproj",
    )(x2d, g, w_bf16, lng, lnb)


MASKED = -1e30
ITEMS_PER_STAGE = 1
TRIP_UNROLL = 8
ROW_CHUNK = 32


def _attn_kernel(q_ref, k_ref, v_ref, o_ref, qm_ref, bias_ref, u_ref, xs_ref, z_ref, a_ref,
                 carry_ref, acc_ref):
    s_len = q_ref.shape[1]
    nsup = s_len // SUPER
    blk = ATT_BLOCK

    @pl.when((pl.program_id(0) == 0) & (pl.program_id(1) == 0))
    def _():
        drow = lax.broadcasted_iota(jnp.int32, (ITEM_ROWS, SUPER), 0)
        dcol = lax.broadcasted_iota(jnp.int32, (ITEM_ROWS, SUPER), 1)
        qpos = ((drow >> 8) << 7) + (drow & (blk - 1))
        bias_ref[0] = jnp.zeros((ITEM_ROWS, SUPER), F32)
        bias_ref[1] = jnp.where(dcol < qpos, 0.0, MASKED).astype(F32)
        krow = lax.broadcasted_iota(jnp.int32, (SUPER, SUPER), 0)
        kcol = lax.broadcasted_iota(jnp.int32, (SUPER, SUPER), 1)
        u_ref[...] = jnp.where(krow >= kcol, -1.0, 0.0).astype(BF16)

    lane = lax.broadcasted_iota(jnp.int32, (blk, LANES), 1)
    first_head = lane < SB_HEAD_DIM
    for m in range(nsup):
        for half in range(2):
            qb = q_ref[0, (2 * m + half) * blk:(2 * m + half + 1) * blk, :]
            zero = jnp.zeros_like(qb)
            qm_ref[m, (2 * half) * blk:(2 * half + 1) * blk, :] = jnp.where(first_head, qb, zero)
            qm_ref[m, (2 * half + 1) * blk:(2 * half + 2) * blk, :] = jnp.where(first_head, zero, qb)
    carry_ref[...] = jnp.zeros_like(carry_ref)
    acc_ref[...] = jnp.zeros_like(acc_ref)

    def key_rows(c):
        off = c * SUPER
        return pl.ds(off if isinstance(off, int) else pl.multiple_of(off, SUPER), SUPER)

    def scores(item, slot):
        m, c = item
        diag = int(m == c) if isinstance(m, int) else jnp.where(m == c, 1, 0)
        zfull = lax.dot_general(qm_ref[m], k_ref[0, key_rows(c), :], (((1,), (1,)), ((), ())),
                                preferred_element_type=F32)
        for rc in range(ITEM_ROWS // ROW_CHUNK):
            rows = slice(rc * ROW_CHUNK, (rc + 1) * ROW_CHUNK)
            z2 = zfull[rows] + bias_ref[diag, rows, :]
            sp = jnp.maximum(z2, 0.0) + jnp.log2(1.0 + jnp.exp2(-jnp.abs(z2)))
            xs_ref[slot, rows, :] = sp.astype(BF16)
            carry = carry_ref[m, rows, :]
            z_ref[slot, rows, :] = z2 + jnp.concatenate([carry, carry], axis=1)
            carry_ref[m, rows, :] = carry - jnp.sum(sp, axis=-1, keepdims=True)

    def weights(slot_in, slot_out):
        st = jnp.dot(xs_ref[slot_in], u_ref[...], preferred_element_type=F32)
        for rc in range(ITEM_ROWS // ROW_CHUNK):
            rows = slice(rc * ROW_CHUNK, (rc + 1) * ROW_CHUNK)
            a_ref[slot_out, rows, :] = jnp.exp2(z_ref[slot_in, rows, :] + st[rows]).astype(BF16)

    def values(item, slot):
        m, c = item
        acc_ref[m] += jnp.dot(a_ref[slot], v_ref[0, key_rows(c), :], preferred_element_type=F32)

    def advance(item):
        m, c = item
        last = c == 0
        return jnp.where(last, m + 1, m), jnp.where(last, m + 1, c - 1)

    def trip(cur, s3_items, s2_items, s1_items):
        prev = 1 - cur
        n = ITEMS_PER_STAGE
        for i, item in enumerate(s2_items):
            weights(prev * n + i, cur * n + i)
        for i, item in enumerate(s1_items):
            scores(item, cur * n + i)
        for i, item in enumerate(s3_items):
            values(item, prev * n + i)

    items = [(m, c) for m in range(nsup) for c in range(m, -1, -1)]
    n_trips = len(items) // ITEMS_PER_STAGE
    assert len(items) % ITEMS_PER_STAGE == 0 and n_trips % 2 == 0
    pair = lambda t: items[ITEMS_PER_STAGE * t:ITEMS_PER_STAGE * (t + 1)] if 0 <= t < n_trips else []

    lead = 2 + (n_trips - 2) % TRIP_UNROLL
    assert lead % 2 == 0 and TRIP_UNROLL % 2 == 0 and n_trips > lead
    for t in range(lead):
        trip(t & 1, pair(t - 2), pair(t - 1), pair(t))

    def steady(_, state):
        s3_items, s2_items, nxt = state
        for u in range(TRIP_UNROLL):
            s1_items = []
            for _ in range(ITEMS_PER_STAGE):
                s1_items.append(nxt)
                nxt = advance(nxt)
            trip(u & 1, s3_items, s2_items, s1_items)
            s3_items, s2_items = s2_items, tuple(s1_items)
        return s3_items, s2_items, nxt

    as_i32 = lambda its: tuple((jnp.int32(m), jnp.int32(c)) for m, c in its)
    lax.fori_loop(0, (n_trips - lead) // TRIP_UNROLL, steady,
                  (as_i32(pair(lead - 2)), as_i32(pair(lead - 1)), as_i32(pair(lead)[:1])[0]))
    trip(0, pair(n_trips - 2), pair(n_trips - 1), [])
    trip(1, pair(n_trips - 1), [], [])

    for m in range(nsup):
        for half in range(2):
            h0 = acc_ref[m, (2 * half) * blk:(2 * half + 1) * blk, :]
            h1 = acc_ref[m, (2 * half + 1) * blk:(2 * half + 2) * blk, :]
            o_ref[0, (2 * m + half) * blk:(2 * m + half + 1) * blk, :] = jnp.where(first_head, h0, h1)


def _attention(proj3d):
    b, s, _ = proj3d.shape
    nsup = s // SUPER
    nslot = 2 * ITEMS_PER_STAGE
    spec = lambda off: pl.BlockSpec((1, s, LANES), lambda bi, p: (bi, 0, off + p))
    return pl.pallas_call(
        _attn_kernel,
        grid=(b, HEAD_PAIRS),
        in_specs=[spec(0), spec(HEAD_PAIRS), spec(2 * HEAD_PAIRS)],
        out_specs=pl.BlockSpec((1, s, LANES), lambda bi, p: (bi, 0, p)),
        out_shape=jax.ShapeDtypeStruct((b, s, SB_WIDTH), F32),
        scratch_shapes=[
            pltpu.VMEM((nsup, ITEM_ROWS, LANES), BF16),
            pltpu.VMEM((2, ITEM_ROWS, SUPER), F32),
            pltpu.VMEM((SUPER, SUPER), BF16),
            pltpu.VMEM((nslot, ITEM_ROWS, SUPER), BF16),
            pltpu.VMEM((nslot, ITEM_ROWS, SUPER), F32),
            pltpu.VMEM((nslot, ITEM_ROWS, SUPER), BF16),
            pltpu.VMEM((nsup, ITEM_ROWS, ATT_BLOCK), F32),
            pltpu.VMEM((nsup, ITEM_ROWS, LANES), F32),
        ],
        compiler_params=pltpu.CompilerParams(
            dimension_semantics=("arbitrary", "arbitrary"), vmem_limit_bytes=VMEM_LIMIT_BYTES),
        name="stickbreak_attn",
    )(proj3d, proj3d, proj3d)


def _mix_mlp_kernel(x_ref, a_ref, u_ref, vn_ref, sgw_ref, sgb_ref,
                    ong_ref, wout_ref, nmg_ref, wup_ref, wdn_ref, nfg_ref, o_ref,
                    mixed_ref):
    tm = x_ref.shape[0]
    vn = vn_ref[...]

    pos_t = lax.broadcasted_iota(jnp.int32, (SG_BLOCK, SG_BLOCK), 0)
    pos_s = lax.broadcasted_iota(jnp.int32, (SG_BLOCK, SG_BLOCK), 1)
    chunk_mask = (pos_s // CHUNK) <= (pos_t // CHUNK)
    lane = lax.broadcasted_iota(jnp.int32, (SG_BLOCK, LANES), 1)
    first = lane < SG_GROUP_DIM
    zero = jnp.zeros((SG_BLOCK, LANES), BF16)
    for pair in range(SG_GROUPS // 2):
        w0 = jnp.where(chunk_mask, sgw_ref[2 * pair], 0.0).astype(BF16)
        w1 = jnp.where(chunk_mask, sgw_ref[2 * pair + 1], 0.0).astype(BF16)
        w_pair = jnp.concatenate([w0, w1], axis=1)
        for r in range(tm // SG_BLOCK):
            vblk = vn[r * SG_BLOCK:(r + 1) * SG_BLOCK, pair * LANES:(pair + 1) * LANES]
            rhs = jnp.concatenate([jnp.where(first, vblk, zero), jnp.where(first, zero, vblk)], axis=0)
            mixed_ref[r * SG_BLOCK:(r + 1) * SG_BLOCK, pair * LANES:(pair + 1) * LANES] = (
                jnp.dot(w_pair, rhs, preferred_element_type=F32)
                + sgb_ref[:, pair * LANES:(pair + 1) * LANES])
    b_out = u_ref[...].astype(F32) * mixed_ref[...]

    ong = ong_ref[...]
    a_n = _rms(a_ref[...], ong[:, :SB_WIDTH]).astype(BF16)
    b_n = _rms(b_out, ong[:, SB_WIDTH:]).astype(BF16)
    mix = jnp.concatenate([a_n, b_n], axis=1)
    x1 = x_ref[...] + jnp.dot(mix, wout_ref[...], preferred_element_type=F32)

    h = _rms(x1, nmg_ref[...]).astype(BF16)
    y = x1
    fc = D_MODEL
    for c in range(D_FF // fc):
        up = jnp.dot(h, wup_ref[:, c * fc:(c + 1) * fc], preferred_element_type=F32)
        act = jnp.square(jnp.maximum(up, 0.0)).astype(BF16)
        y = y + jnp.dot(act, wdn_ref[c * fc:(c + 1) * fc, :], preferred_element_type=F32)
    o_ref[...] = _rms(y, nfg_ref[...])


def _mix_mlp(x2d, a2d, proj2d, sgw, sgb_full, ong, wout, nmg, wup, wdn, nfg, tm):
    n = x2d.shape[0]
    const = lambda shape: pl.BlockSpec(shape, lambda i: (0,) * len(shape),
                                       pipeline_mode=pl.Buffered(1))
    u_blk = 3 * SB_WIDTH // SG_WIDTH
    return pl.pallas_call(
        _mix_mlp_kernel,
        grid=(n // tm,),
        in_specs=[
            pl.BlockSpec((tm, D_MODEL), lambda i: (i, 0)),
            pl.BlockSpec((tm, SB_WIDTH), lambda i: (i, 0)),
            pl.BlockSpec((tm, SG_WIDTH), lambda i: (i, u_blk)),
            pl.BlockSpec((tm, SG_WIDTH), lambda i: (i, u_blk + 1)),
            const((SG_GROUPS, SG_BLOCK, SG_BLOCK)), const((SG_BLOCK, SG_WIDTH)),
            const((1, MIX_WIDTH)), const((MIX_WIDTH, D_MODEL)),
            const((1, D_MODEL)), const((D_MODEL, D_FF)), const((D_FF, D_MODEL)),
            const((1, D_MODEL)),
        ],
        out_specs=pl.BlockSpec((tm, D_MODEL), lambda i: (i, 0)),
        out_shape=jax.ShapeDtypeStruct((n, D_MODEL), F32),
        scratch_shapes=[pltpu.VMEM((tm, SG_WIDTH), F32)],
        compiler_params=pltpu.CompilerParams(
            dimension_semantics=("arbitrary",), vmem_limit_bytes=VMEM_LIMIT_BYTES),
        name="mix_mlp",
    )(x2d, a2d, proj2d, proj2d, sgw, sgb_full, ong, wout, nmg, wup, wdn, nfg)


def kernel(x, norm_mix_g, w_in, sg_ln_g, sg_ln_b, sg_w, sg_b, out_norm_g, w_out,
           norm_mlp_g, w_up, w_down, norm_final_g):
    b, s, d = x.shape
    n = b * s
    assert w_in.shape[0] == 1, "the final norm is fused into the (single) layer's last kernel"
    layer = 0
    x2d = x.reshape(n, d)
    proj = _inproj(x2d, norm_mix_g[layer][None], w_in[layer].astype(BF16),
                   sg_ln_g[layer][None], sg_ln_b[layer][None], tm=512)
    a_out = _attention(proj.reshape(b, s, IN_WIDTH))
    sgb_full = jnp.repeat(jnp.transpose(sg_b[layer]), SG_GROUP_DIM, axis=1)
    out = _mix_mlp(
        x2d, a_out.reshape(n, SB_WIDTH), proj, sg_w[layer], sgb_full,
        out_norm_g[layer][None], w_out[layer].astype(BF16), norm_mlp_g[layer][None],
        w_up[layer].astype(BF16), w_down[layer].astype(BF16), norm_final_g[None], tm=512)
    return out.reshape(b, s, d)
```

```python
import jax
import jax.numpy as jnp
from jax import lax
from jax.experimental import pallas as pl
from jax.experimental.pallas import tpu as pltpu

D_MODEL = 1024
CHUNK = 64
SB_HEADS = 8
SB_HEAD_DIM = 64
SB_WIDTH = SB_HEADS * SB_HEAD_DIM
SG_GROUPS = 8
SG_GROUP_DIM = 64
SG_WIDTH = SG_GROUPS * SG_GROUP_DIM
SG_BLOCK = 128
MIX_WIDTH = SB_WIDTH + SG_WIDTH
IN_WIDTH = 3 * SB_WIDTH + 2 * SG_WIDTH
D_FF = 4 * D_MODEL
EPS = 1e-6

LANES = 128
ATT_BLOCK = 128
SUPER = 2 * ATT_BLOCK
ITEM_ROWS = 2 * SUPER
HEAD_PAIRS = SB_HEADS * SB_HEAD_DIM // LANES
Q_SCALE = 1.4426950408889634 / (SB_HEAD_DIM ** 0.5)
VMEM_LIMIT_BYTES = 56 * 1024 * 1024

F32 = jnp.float32
BF16 = jnp.bfloat16


def _rms(x, g):
    ms = jnp.mean(x * x, axis=-1, keepdims=True)
    return x * lax.rsqrt(ms + EPS) * g


def _inproj_kernel(x_ref, g_ref, w_ref, lng_ref, lnb_ref, o_ref):
    qkv = 3 * SB_WIDTH
    half = SG_WIDTH // 2
    h = _rms(x_ref[...], g_ref[...]).astype(BF16)

    def cols(lo, width):
        return jnp.dot(h, w_ref[:, lo:lo + width], preferred_element_type=F32)

    vacts = []
    for step, part in enumerate((2, 3, 0, 1)):
        lo = qkv + part * half
        act = jax.nn.gelu(cols(lo, half))
        if part < 2:
            o_ref[:, lo:lo + half] = act.astype(o_ref.dtype)
        else:
            vacts.append(act)
        if part == 3:
            vact = jnp.concatenate(vacts, axis=1)
            mu = jnp.mean(vact, axis=-1, keepdims=True)
            vc = vact - mu
            var = jnp.mean(vc * vc, axis=-1, keepdims=True)
            vn = vc * lax.rsqrt(var + EPS) * lng_ref[...] + lnb_ref[...]
            o_ref[:, qkv + SG_WIDTH:] = vn.astype(o_ref.dtype)
        if step < 3:
            p = cols(step * SB_WIDTH, SB_WIDTH)
            if step == 0:
                p = p * Q_SCALE
            o_ref[:, step * SB_WIDTH:(step + 1) * SB_WIDTH] = p.astype(o_ref.dtype)


def _inproj(x2d, g, w_bf16, lng, lnb, tm):
    n = x2d.shape[0]
    const = lambda shape: pl.BlockSpec(shape, lambda i: (0, 0), pipeline_mode=pl.Buffered(1))
    return pl.pallas_call(
        _inproj_kernel,
        grid=(n // tm,),
        in_specs=[
            pl.BlockSpec((tm, D_MODEL), lambda i: (i, 0)),
            const((1, D_MODEL)), const((D_MODEL, IN_WIDTH)),
            const((1, SG_WIDTH)), const((1, SG_WIDTH)),
        ],
        out_specs=pl.BlockSpec((tm, IN_WIDTH), lambda i: (i, 0)),
        out_shape=jax.ShapeDtypeStruct((n, IN_WIDTH), BF16),
        compiler_params=pltpu.CompilerParams(
            dimension_semantics=("arbitrary",), vmem_limit_bytes=VMEM_LIMIT_BYTES),
        name="inproj",
    )(x2d, g, w_bf16, lng, lnb)


MASKED = -1e30
ROW_CHUNK = 32


def _attn_kernel(q_ref, k_ref, v_ref, o_ref, qm_ref, bias_ref, u_ref, xs_ref, z_ref, a_ref,
                 carry_ref, acc_ref):
    s_len = q_ref.shape[1]
    nsup = s_len // SUPER
    blk = ATT_BLOCK

    @pl.when((pl.program_id(0) == 0) & (pl.program_id(1) == 0))
    def _():
        drow = lax.broadcasted_iota(jnp.int32, (ITEM_ROWS, SUPER), 0)
        dcol = lax.broadcasted_iota(jnp.int32, (ITEM_ROWS, SUPER), 1)
        qpos = ((drow >> 8) << 7) + (drow & (blk - 1))
        bias_ref[...] = jnp.where(dcol < qpos, 0.0, MASKED).astype(F32)
        krow = lax.broadcasted_iota(jnp.int32, (SUPER, SUPER), 0)
        kcol = lax.broadcasted_iota(jnp.int32, (SUPER, SUPER), 1)
        u_ref[...] = jnp.where(krow >= kcol, -1.0, 0.0).astype(BF16)

    lane = lax.broadcasted_iota(jnp.int32, (blk, LANES), 1)
    first_head = lane < SB_HEAD_DIM
    for m in range(nsup):
        for half in range(2):
            qb = q_ref[0, (2 * m + half) * blk:(2 * m + half + 1) * blk, :]
            zero = jnp.zeros_like(qb)
            qm_ref[m, (2 * half) * blk:(2 * half + 1) * blk, :] = jnp.where(first_head, qb, zero)
            qm_ref[m, (2 * half + 1) * blk:(2 * half + 2) * blk, :] = jnp.where(first_head, zero, qb)
    carry_ref[...] = jnp.zeros_like(carry_ref)
    acc_ref[...] = jnp.zeros_like(acc_ref)

    def key_rows(c):
        return slice(c * SUPER, (c + 1) * SUPER)

    def scores(item, slot):
        m, c = item
        zfull = lax.dot_general(qm_ref[m], k_ref[0, key_rows(c), :], (((1,), (1,)), ((), ())),
                                preferred_element_type=F32)
        for rc in range(ITEM_ROWS // ROW_CHUNK):
            rows = slice(rc * ROW_CHUNK, (rc + 1) * ROW_CHUNK)
            live = blk if (m == c and rc < SUPER // ROW_CHUNK) else SUPER
            z2 = zfull[rows, :live]
            if m == c:
                z2 = z2 + bias_ref[rows, :live]
            sp = jnp.maximum(z2, 0.0) + jnp.log2(1.0 + jnp.exp2(-jnp.abs(z2)))
            xs_ref[slot, rows, :live] = sp.astype(BF16)
            carry = carry_ref[m, rows, :]
            z_ref[slot, rows, :live] = z2 + jnp.concatenate([carry] * (live // blk), axis=1)
            carry_ref[m, rows, :] = carry - jnp.sum(sp, axis=-1, keepdims=True)
            if live < SUPER:
                xs_ref[slot, rows, live:] = jnp.zeros((ROW_CHUNK, SUPER - live), BF16)
                z_ref[slot, rows, live:] = jnp.full((ROW_CHUNK, SUPER - live), MASKED, F32)

    def weights(slot_in, slot_out):
        st = jnp.dot(xs_ref[slot_in], u_ref[...], preferred_element_type=F32)
        for rc in range(ITEM_ROWS // ROW_CHUNK):
            rows = slice(rc * ROW_CHUNK, (rc + 1) * ROW_CHUNK)
            a_ref[slot_out, rows, :] = jnp.exp2(z_ref[slot_in, rows, :] + st[rows]).astype(BF16)

    def values(item, slot):
        m, c = item
        acc_ref[m] += jnp.dot(a_ref[slot], v_ref[0, key_rows(c), :], preferred_element_type=F32)

    items = [(m, c) for m in range(nsup) for c in range(m, -1, -1)]
    for t in range(len(items) + 2):
        cur, prev = t % 2, (t + 1) % 2
        if 0 <= t - 1 < len(items):
            weights(prev, cur)
        if t < len(items):
            scores(items[t], cur)
        if 0 <= t - 2 < len(items):
            values(items[t - 2], prev)

    for m in range(nsup):
        for half in range(2):
            h0 = acc_ref[m, (2 * half) * blk:(2 * half + 1) * blk, :]
            h1 = acc_ref[m, (2 * half + 1) * blk:(2 * half + 2) * blk, :]
            o_ref[0, (2 * m + half) * blk:(2 * m + half + 1) * blk, :] = jnp.where(first_head, h0, h1)


def _attention(proj3d):
    b, s, _ = proj3d.shape
    nsup = s // SUPER
    nslot = 2
    spec = lambda off: pl.BlockSpec((1, s, LANES), lambda bi, p: (bi, 0, off + p))
    return pl.pallas_call(
        _attn_kernel,
        grid=(b, HEAD_PAIRS),
        in_specs=[spec(0), spec(HEAD_PAIRS), spec(2 * HEAD_PAIRS)],
        out_specs=pl.BlockSpec((1, s, LANES), lambda bi, p: (bi, 0, p)),
        out_shape=jax.ShapeDtypeStruct((b, s, SB_WIDTH), F32),
        scratch_shapes=[
            pltpu.VMEM((nsup, ITEM_ROWS, LANES), BF16),
            pltpu.VMEM((ITEM_ROWS, SUPER), F32),
            pltpu.VMEM((SUPER, SUPER), BF16),
            pltpu.VMEM((nslot, ITEM_ROWS, SUPER), BF16),
            pltpu.VMEM((nslot, ITEM_ROWS, SUPER), F32),
            pltpu.VMEM((nslot, ITEM_ROWS, SUPER), BF16),
            pltpu.VMEM((nsup, ITEM_ROWS, ATT_BLOCK), F32),
            pltpu.VMEM((nsup, ITEM_ROWS, LANES), F32),
        ],
        compiler_params=pltpu.CompilerParams(
            dimension_semantics=("arbitrary", "arbitrary"), vmem_limit_bytes=VMEM_LIMIT_BYTES),
        name="stickbreak_attn",
    )(proj3d, proj3d, proj3d)


def _mix_mlp_kernel(x_ref, a_ref, u_ref, vn_ref, sgw_ref, sgb_ref,
                    ong_ref, wout_ref, nmg_ref, wup_ref, wdn_ref, nfg_ref, o_ref,
                    mixed_ref):
    tm = x_ref.shape[0]
    vn = vn_ref[...]

    pos_t = lax.broadcasted_iota(jnp.int32, (SG_BLOCK, SG_BLOCK), 0)
    pos_s = lax.broadcasted_iota(jnp.int32, (SG_BLOCK, SG_BLOCK), 1)
    chunk_mask = (pos_s // CHUNK) <= (pos_t // CHUNK)
    lane = lax.broadcasted_iota(jnp.int32, (SG_BLOCK, LANES), 1)
    first = lane < SG_GROUP_DIM
    zero = jnp.zeros((SG_BLOCK, LANES), BF16)
    for pair in range(SG_GROUPS // 2):
        w0 = jnp.where(chunk_mask, sgw_ref[2 * pair], 0.0).astype(BF16)
        w1 = jnp.where(chunk_mask, sgw_ref[2 * pair + 1], 0.0).astype(BF16)
        w_pair = jnp.concatenate([w0, w1], axis=1)
        for r in range(tm // SG_BLOCK):
            vblk = vn[r * SG_BLOCK:(r + 1) * SG_BLOCK, pair * LANES:(pair + 1) * LANES]
            rhs = jnp.concatenate([jnp.where(first, vblk, zero), jnp.where(first, zero, vblk)], axis=0)
            mixed_ref[r * SG_BLOCK:(r + 1) * SG_BLOCK, pair * LANES:(pair + 1) * LANES] = (
                jnp.dot(w_pair, rhs, preferred_element_type=F32)
                + sgb_ref[:, pair * LANES:(pair + 1) * LANES])
    b_out = u_ref[...].astype(F32) * mixed_ref[...]

    ong = ong_ref[...]
    a_n = _rms(a_ref[...], ong[:, :SB_WIDTH]).astype(BF16)
    b_n = _rms(b_out, ong[:, SB_WIDTH:]).astype(BF16)
    mix = jnp.concatenate([a_n, b_n], axis=1)
    x1 = x_ref[...] + jnp.dot(mix, wout_ref[...], preferred_element_type=F32)

    h = _rms(x1, nmg_ref[...]).astype(BF16)
    y = x1
    fc = D_MODEL
    for c in range(D_FF // fc):
        up = jnp.dot(h, wup_ref[:, c * fc:(c + 1) * fc], preferred_element_type=F32)
        act = jnp.square(jnp.maximum(up, 0.0)).astype(BF16)
        y = y + jnp.dot(act, wdn_ref[c * fc:(c + 1) * fc, :], preferred_element_type=F32)
    o_ref[...] = _rms(y, nfg_ref[...])


def _mix_mlp(x2d, a2d, proj2d, sgw, sgb_full, ong, wout, nmg, wup, wdn, nfg, tm):
    n = x2d.shape[0]
    const = lambda shape: pl.BlockSpec(shape, lambda i: (0,) * len(shape),
                                       pipeline_mode=pl.Buffered(1))
    u_blk = 3 * SB_WIDTH // SG_WIDTH
    return pl.pallas_call(
        _mix_mlp_kernel,
        grid=(n // tm,),
        in_specs=[
            pl.BlockSpec((tm, D_MODEL), lambda i: (i, 0)),
            pl.BlockSpec((tm, SB_WIDTH), lambda i: (i, 0)),
            pl.BlockSpec((tm, SG_WIDTH), lambda i: (i, u_blk)),
            pl.BlockSpec((tm, SG_WIDTH), lambda i: (i, u_blk + 1)),
            const((SG_GROUPS, SG_BLOCK, SG_BLOCK)), const((SG_BLOCK, SG_WIDTH)),
            const((1, MIX_WIDTH)), const((MIX_WIDTH, D_MODEL)),
            const((1, D_MODEL)), const((D_MODEL, D_FF)), const((D_FF, D_MODEL)),
            const((1, D_MODEL)),
        ],
        out_specs=pl.BlockSpec((tm, D_MODEL), lambda i: (i, 0)),
        out_shape=jax.ShapeDtypeStruct((n, D_MODEL), F32),
        scratch_shapes=[pltpu.VMEM((tm, SG_WIDTH), F32)],
        compiler_params=pltpu.CompilerParams(
            dimension_semantics=("arbitrary",), vmem_limit_bytes=VMEM_LIMIT_BYTES),
        name="mix_mlp",
    )(x2d, a2d, proj2d, proj2d, sgw, sgb_full, ong, wout, nmg, wup, wdn, nfg)


def kernel(x, norm_mix_g, w_in, sg_ln_g, sg_ln_b, sg_w, sg_b, out_norm_g, w_out,
           norm_mlp_g, w_up, w_down, norm_final_g):
    b, s, d = x.shape
    n = b * s
    assert w_in.shape[0] == 1, "the final norm is fused into the (single) layer's last kernel"
    layer = 0
    x2d = x.reshape(n, d)
    proj = _inproj(x2d, norm_mix_g[layer][None], w_in[layer].astype(BF16),
                   sg_ln_g[layer][None], sg_ln_b[layer][None], tm=512)
    a_out = _attention(proj.reshape(b, s, IN_WIDTH))
    sgb_full = jnp.repeat(jnp.transpose(sg_b[layer]), SG_GROUP_DIM, axis=1)
    out = _mix_mlp(
        x2d, a_out.reshape(n, SB_WIDTH), proj, sg_w[layer], sgb_full,
        out_norm_g[layer][None], w_out[layer].astype(BF16), norm_mlp_g[layer][None],
        w_up[layer].astype(BF16), w_down[layer].astype(BF16), norm_final_g[None], tm=512)
    return out.reshape(b, s, d)
```

```python
import jax
import jax.numpy as jnp
from jax import lax
from jax.experimental import pallas as pl
from jax.experimental.pallas import tpu as pltpu

D_MODEL = 1024
CHUNK = 64
SB_HEADS = 8
SB_HEAD_DIM = 64
SB_WIDTH = SB_HEADS * SB_HEAD_DIM
SG_GROUPS = 8
SG_GROUP_DIM = 64
SG_WIDTH = SG_GROUPS * SG_GROUP_DIM
SG_BLOCK = 128
MIX_WIDTH = SB_WIDTH + SG_WIDTH
IN_WIDTH = 3 * SB_WIDTH + 2 * SG_WIDTH
D_FF = 4 * D_MODEL
EPS = 1e-6

LANES = 128
ATT_BLOCK = 128
SUPER = 2 * ATT_BLOCK
ITEM_ROWS = 2 * SUPER
HEAD_PAIRS = SB_HEADS * SB_HEAD_DIM // LANES
Q_SCALE = 1.4426950408889634 / (SB_HEAD_DIM ** 0.5)
VMEM_LIMIT_BYTES = 56 * 1024 * 1024

F32 = jnp.float32
BF16 = jnp.bfloat16


def _rms(x, g):
    ms = jnp.mean(x * x, axis=-1, keepdims=True)
    return x * lax.rsqrt(ms + EPS) * g


def _inproj_kernel(x_ref, g_ref, w_ref, lng_ref, lnb_ref, o_ref):
    qkv = 3 * SB_WIDTH
    half = SG_WIDTH // 2
    h = _rms(x_ref[...], g_ref[...]).astype(BF16)

    def cols(lo, width):
        return jnp.dot(h, w_ref[:, lo:lo + width], preferred_element_type=F32)

    vacts = []
    for step, part in enumerate((2, 3, 0, 1)):
        lo = qkv + part * half
        act = jax.nn.gelu(cols(lo, half))
        if part < 2:
            o_ref[:, lo:lo + half] = act.astype(o_ref.dtype)
        else:
            vacts.append(act)
        if part == 3:
            vact = jnp.concatenate(vacts, axis=1)
            mu = jnp.mean(vact, axis=-1, keepdims=True)
            vc = vact - mu
            var = jnp.mean(vc * vc, axis=-1, keepdims=True)
            vn = vc * lax.rsqrt(var + EPS) * lng_ref[...] + lnb_ref[...]
            o_ref[:, qkv + SG_WIDTH:] = vn.astype(o_ref.dtype)
        if step < 3:
            p = cols(step * SB_WIDTH, SB_WIDTH)
            if step == 0:
                p = p * Q_SCALE
            o_ref[:, step * SB_WIDTH:(step + 1) * SB_WIDTH] = p.astype(o_ref.dtype)


def _inproj(x2d, g, w_bf16, lng, lnb, tm):
    n = x2d.shape[0]
    const = lambda shape: pl.BlockSpec(shape, lambda i: (0, 0), pipeline_mode=pl.Buffered(1))
    return pl.pallas_call(
        _inproj_kernel,
        grid=(n // tm,),
        in_specs=[
            pl.BlockSpec((tm, D_MODEL), lambda i: (i, 0)),
            const((1, D_MODEL)), const((D_MODEL, IN_WIDTH)),
            const((1, SG_WIDTH)), const((1, SG_WIDTH)),
        ],
        out_specs=pl.BlockSpec((tm, IN_WIDTH), lambda i: (i, 0)),
        out_shape=jax.ShapeDtypeStruct((n, IN_WIDTH), BF16),
        compiler_params=pltpu.CompilerParams(
            dimension_semantics=("arbitrary",), vmem_limit_bytes=VMEM_LIMIT_BYTES),
        name="inproj",
    )(x2d, g, w_bf16, lng, lnb)


MASKED = -1e30
ROW_CHUNK = 32


def _attn_kernel(q_ref, k_ref, v_ref, o_ref, qm_ref, bias_ref, u_ref, xs_ref, z_ref, a_ref,
                 carry_ref, acc_ref):
    s_len = q_ref.shape[1]
    nsup = s_len // SUPER
    blk = ATT_BLOCK

    @pl.when((pl.program_id(0) == 0) & (pl.program_id(1) == 0))
    def _():
        drow = lax.broadcasted_iota(jnp.int32, (ITEM_ROWS, SUPER), 0)
        dcol = lax.broadcasted_iota(jnp.int32, (ITEM_ROWS, SUPER), 1)
        qpos = ((drow >> 8) << 7) + (drow & (blk - 1))
        bias_ref[...] = jnp.where(dcol < qpos, 0.0, MASKED).astype(F32)
        krow = lax.broadcasted_iota(jnp.int32, (SUPER, SUPER), 0)
        kcol = lax.broadcasted_iota(jnp.int32, (SUPER, SUPER), 1)
        u_ref[...] = jnp.where(krow >= kcol, -1.0, 0.0).astype(BF16)

    lane = lax.broadcasted_iota(jnp.int32, (blk, LANES), 1)
    first_head = lane < SB_HEAD_DIM
    for m in range(nsup):
        for half in range(2):
            qb = q_ref[0, (2 * m + half) * blk:(2 * m + half + 1) * blk, :]
            zero = jnp.zeros_like(qb)
            qm_ref[m, (2 * half) * blk:(2 * half + 1) * blk, :] = jnp.where(first_head, qb, zero)
            qm_ref[m, (2 * half + 1) * blk:(2 * half + 2) * blk, :] = jnp.where(first_head, zero, qb)
    def key_rows(c):
        return slice(c * SUPER, (c + 1) * SUPER)

    def scores(item, slot):
        m, c = item
        zfull = lax.dot_general(qm_ref[m], k_ref[0, key_rows(c), :], (((1,), (1,)), ((), ())),
                                preferred_element_type=F32)
        for rc in range(ITEM_ROWS // ROW_CHUNK):
            rows = slice(rc * ROW_CHUNK, (rc + 1) * ROW_CHUNK)
            live = blk if (m == c and rc < SUPER // ROW_CHUNK) else SUPER
            z2 = zfull[rows, :live]
            if m == c:
                z2 = z2 + bias_ref[rows, :live]
            sp = jnp.maximum(z2, 0.0) + jnp.log2(1.0 + jnp.exp2(-jnp.abs(z2)))
            xs_ref[slot, rows, :live] = sp.astype(BF16)
            total = jnp.sum(sp, axis=-1, keepdims=True)
            if m == c:
                z_ref[slot, rows, :live] = z2
                carry_ref[m, rows, :] = jnp.broadcast_to(-total, (ROW_CHUNK, blk))
            else:
                carry = carry_ref[m, rows, :]
                z_ref[slot, rows, :live] = z2 + jnp.concatenate([carry, carry], axis=1)
                carry_ref[m, rows, :] = carry - total
            if live < SUPER:
                xs_ref[slot, rows, live:] = jnp.zeros((ROW_CHUNK, SUPER - live), BF16)
                z_ref[slot, rows, live:] = jnp.full((ROW_CHUNK, SUPER - live), MASKED, F32)

    def weights(slot_in, slot_out):
        st = jnp.dot(xs_ref[slot_in], u_ref[...], preferred_element_type=F32)
        for rc in range(ITEM_ROWS // ROW_CHUNK):
            rows = slice(rc * ROW_CHUNK, (rc + 1) * ROW_CHUNK)
            a_ref[slot_out, rows, :] = jnp.exp2(z_ref[slot_in, rows, :] + st[rows]).astype(BF16)

    def values(item, slot):
        m, c = item
        av = jnp.dot(a_ref[slot], v_ref[0, key_rows(c), :], preferred_element_type=F32)
        if m == c:
            acc_ref[m] = av
        else:
            acc_ref[m] += av

    items = [(m, c) for m in range(nsup) for c in range(m, -1, -1)]
    for t in range(len(items) + 2):
        cur, prev = t % 2, (t + 1) % 2
        if t < len(items):
            scores(items[t], cur)
        if 0 <= t - 2 < len(items):
            values(items[t - 2], prev)
        if 0 <= t - 1 < len(items):
            weights(prev, cur)

    for m in range(nsup):
        for half in range(2):
            h0 = acc_ref[m, (2 * half) * blk:(2 * half + 1) * blk, :]
            h1 = acc_ref[m, (2 * half + 1) * blk:(2 * half + 2) * blk, :]
            o_ref[0, (2 * m + half) * blk:(2 * m + half + 1) * blk, :] = jnp.where(first_head, h0, h1)


def _attention(proj3d):
    b, s, _ = proj3d.shape
    nsup = s // SUPER
    nslot = 2
    spec = lambda off: pl.BlockSpec((1, s, LANES), lambda bi, p: (bi, 0, off + p))
    return pl.pallas_call(
        _attn_kernel,
        grid=(b, HEAD_PAIRS),
        in_specs=[spec(0), spec(HEAD_PAIRS), spec(2 * HEAD_PAIRS)],
        out_specs=pl.BlockSpec((1, s, LANES), lambda bi, p: (bi, 0, p)),
        out_shape=jax.ShapeDtypeStruct((b, s, SB_WIDTH), F32),
        scratch_shapes=[
            pltpu.VMEM((nsup, ITEM_ROWS, LANES), BF16),
            pltpu.VMEM((ITEM_ROWS, SUPER), F32),
            pltpu.VMEM((SUPER, SUPER), BF16),
            pltpu.VMEM((nslot, ITEM_ROWS, SUPER), BF16),
            pltpu.VMEM((nslot, ITEM_ROWS, SUPER), F32),
            pltpu.VMEM((nslot, ITEM_ROWS, SUPER), BF16),
            pltpu.VMEM((nsup, ITEM_ROWS, ATT_BLOCK), F32),
            pltpu.VMEM((nsup, ITEM_ROWS, LANES), F32),
        ],
        compiler_params=pltpu.CompilerParams(
            dimension_semantics=("arbitrary", "arbitrary"), vmem_limit_bytes=VMEM_LIMIT_BYTES),
        name="stickbreak_attn",
    )(proj3d, proj3d, proj3d)


def _mix_mlp_kernel(x_ref, a_ref, u_ref, vn_ref, sgw_ref, sgb_ref,
                    ong_ref, wout_ref, nmg_ref, wup_ref, wdn_ref, nfg_ref, o_ref,
                    mixed_ref):
    tm = x_ref.shape[0]
    vn = vn_ref[...]

    pos_t = lax.broadcasted_iota(jnp.int32, (SG_BLOCK, SG_BLOCK), 0)
    pos_s = lax.broadcasted_iota(jnp.int32, (SG_BLOCK, SG_BLOCK), 1)
    chunk_mask = (pos_s // CHUNK) <= (pos_t // CHUNK)
    lane = lax.broadcasted_iota(jnp.int32, (SG_BLOCK, LANES), 1)
    first = lane < SG_GROUP_DIM
    zero = jnp.zeros((SG_BLOCK, LANES), BF16)
    for pair in range(SG_GROUPS // 2):
        w0 = jnp.where(chunk_mask, sgw_ref[2 * pair], 0.0).astype(BF16)
        w1 = jnp.where(chunk_mask, sgw_ref[2 * pair + 1], 0.0).astype(BF16)
        w_pair = jnp.concatenate([w0, w1], axis=1)
        for r in range(tm // SG_BLOCK):
            vblk = vn[r * SG_BLOCK:(r + 1) * SG_BLOCK, pair * LANES:(pair + 1) * LANES]
            rhs = jnp.concatenate([jnp.where(first, vblk, zero), jnp.where(first, zero, vblk)], axis=0)
            mixed_ref[r * SG_BLOCK:(r + 1) * SG_BLOCK, pair * LANES:(pair + 1) * LANES] = (
                jnp.dot(w_pair, rhs, preferred_element_type=F32)
                + sgb_ref[:, pair * LANES:(pair + 1) * LANES])
    b_out = u_ref[...].astype(F32) * mixed_ref[...]

    ong = ong_ref[...]
    a_n = _rms(a_ref[...], ong[:, :SB_WIDTH]).astype(BF16)
    b_n = _rms(b_out, ong[:, SB_WIDTH:]).astype(BF16)
    mix = jnp.concatenate([a_n, b_n], axis=1)
    x1 = x_ref[...] + jnp.dot(mix, wout_ref[...], preferred_element_type=F32)

    h = _rms(x1, nmg_ref[...]).astype(BF16)
    y = x1
    fc = D_MODEL
    for c in range(D_FF // fc):
        up = jnp.dot(h, wup_ref[:, c * fc:(c + 1) * fc], preferred_element_type=F32)
        act = jnp.square(jnp.maximum(up, 0.0)).astype(BF16)
        y = y + jnp.dot(act, wdn_ref[c * fc:(c + 1) * fc, :], preferred_element_type=F32)
    o_ref[...] = _rms(y, nfg_ref[...])


def _mix_mlp(x2d, a2d, proj2d, sgw, sgb_full, ong, wout, nmg, wup, wdn, nfg, tm):
    n = x2d.shape[0]
    const = lambda shape: pl.BlockSpec(shape, lambda i: (0,) * len(shape),
                                       pipeline_mode=pl.Buffered(1))
    u_blk = 3 * SB_WIDTH // SG_WIDTH
    return pl.pallas_call(
        _mix_mlp_kernel,
        grid=(n // tm,),
        in_specs=[
            pl.BlockSpec((tm, D_MODEL), lambda i: (i, 0)),
            pl.BlockSpec((tm, SB_WIDTH), lambda i: (i, 0)),
            pl.BlockSpec((tm, SG_WIDTH), lambda i: (i, u_blk)),
            pl.BlockSpec((tm, SG_WIDTH), lambda i: (i, u_blk + 1)),
            const((SG_GROUPS, SG_BLOCK, SG_BLOCK)), const((SG_BLOCK, SG_WIDTH)),
            const((1, MIX_WIDTH)), const((MIX_WIDTH, D_MODEL)),
            const((1, D_MODEL)), const((D_MODEL, D_FF)), const((D_FF, D_MODEL)),
            const((1, D_MODEL)),
        ],
        out_specs=pl.BlockSpec((tm, D_MODEL), lambda i: (i, 0)),
        out_shape=jax.ShapeDtypeStruct((n, D_MODEL), F32),
        scratch_shapes=[pltpu.VMEM((tm, SG_WIDTH), F32)],
        compiler_params=pltpu.CompilerParams(
            dimension_semantics=("arbitrary",), vmem_limit_bytes=VMEM_LIMIT_BYTES),
        name="mix_mlp",
    )(x2d, a2d, proj2d, proj2d, sgw, sgb_full, ong, wout, nmg, wup, wdn, nfg)


def kernel(x, norm_mix_g, w_in, sg_ln_g, sg_ln_b, sg_w, sg_b, out_norm_g, w_out,
           norm_mlp_g, w_up, w_down, norm_final_g):
    b, s, d = x.shape
    n = b * s
    assert w_in.shape[0] == 1, "the final norm is fused into the (single) layer's last kernel"
    layer = 0
    x2d = x.reshape(n, d)
    proj = _inproj(x2d, norm_mix_g[layer][None], w_in[layer].astype(BF16),
                   sg_ln_g[layer][None], sg_ln_b[layer][None], tm=512)
    a_out = _attention(proj.reshape(b, s, IN_WIDTH))
    sgb_full = jnp.repeat(jnp.transpose(sg_b[layer]), SG_GROUP_DIM, axis=1)
    out = _mix_mlp(
        x2d, a_out.reshape(n, SB_WIDTH), proj, sg_w[layer], sgb_full,
        out_norm_g[layer][None], w_out[layer].astype(BF16), norm_mlp_g[layer][None],
        w_up[layer].astype(BF16), w_down[layer].astype(BF16), norm_final_g[None], tm=512)
    return out.reshape(b, s, d)
```

```python
import jax
import jax.numpy as jnp
from jax import lax
from jax.experimental import pallas as pl
from jax.experimental.pallas import tpu as pltpu

D_MODEL = 1024
CHUNK = 64
SB_HEADS = 8
SB_HEAD_DIM = 64
SB_WIDTH = SB_HEADS * SB_HEAD_DIM
SG_GROUPS = 8
SG_GROUP_DIM = 64
SG_WIDTH = SG_GROUPS * SG_GROUP_DIM
SG_BLOCK = 128
MIX_WIDTH = SB_WIDTH + SG_WIDTH
IN_WIDTH = 3 * SB_WIDTH + 2 * SG_WIDTH
D_FF = 4 * D_MODEL
EPS = 1e-6

LANES = 128
ATT_BLOCK = 128
SUPER = 2 * ATT_BLOCK
ITEM_ROWS = 2 * SUPER
HEAD_PAIRS = SB_HEADS * SB_HEAD_DIM // LANES
Q_SCALE = 1.4426950408889634 / (SB_HEAD_DIM ** 0.5)
VMEM_LIMIT_BYTES = 56 * 1024 * 1024

F32 = jnp.float32
BF16 = jnp.bfloat16


def _rms(x, g):
    ms = jnp.mean(x * x, axis=-1, keepdims=True)
    return x * lax.rsqrt(ms + EPS) * g


def _inproj_kernel(x_ref, g_ref, w_ref, lng_ref, lnb_ref, o_ref):
    qkv = 3 * SB_WIDTH
    half = SG_WIDTH // 2
    h = _rms(x_ref[...], g_ref[...]).astype(BF16)

    def cols(lo, width):
        return jnp.dot(h, w_ref[:, lo:lo + width], preferred_element_type=F32)

    vacts = []
    for step, part in enumerate((2, 3, 0, 1)):
        lo = qkv + part * half
        act = jax.nn.gelu(cols(lo, half))
        if part < 2:
            o_ref[:, lo:lo + half] = act.astype(o_ref.dtype)
        else:
            vacts.append(act)
        if part == 3:
            vact = jnp.concatenate(vacts, axis=1)
            mu = jnp.mean(vact, axis=-1, keepdims=True)
            vc = vact - mu
            var = jnp.mean(vc * vc, axis=-1, keepdims=True)
            vn = vc * lax.rsqrt(var + EPS) * lng_ref[...] + lnb_ref[...]
            o_ref[:, qkv + SG_WIDTH:] = vn.astype(o_ref.dtype)
        if step < 3:
            p = cols(step * SB_WIDTH, SB_WIDTH)
            if step == 0:
                p = p * Q_SCALE
            o_ref[:, step * SB_WIDTH:(step + 1) * SB_WIDTH] = p.astype(o_ref.dtype)


def _inproj(x2d, g, w_bf16, lng, lnb, tm):
    n = x2d.shape[0]
    const = lambda shape: pl.BlockSpec(shape, lambda i: (0, 0), pipeline_mode=pl.Buffered(1))
    return pl.pallas_call(
        _inproj_kernel,
        grid=(n // tm,),
        in_specs=[
            pl.BlockSpec((tm, D_MODEL), lambda i: (i, 0)),
            const((1, D_MODEL)), const((D_MODEL, IN_WIDTH)),
            const((1, SG_WIDTH)), const((1, SG_WIDTH)),
        ],
        out_specs=pl.BlockSpec((tm, IN_WIDTH), lambda i: (i, 0)),
        out_shape=jax.ShapeDtypeStruct((n, IN_WIDTH), BF16),
        compiler_params=pltpu.CompilerParams(
            dimension_semantics=("arbitrary",), vmem_limit_bytes=VMEM_LIMIT_BYTES),
        name="inproj",
    )(x2d, g, w_bf16, lng, lnb)


MASKED = -1e30
ROW_CHUNK = 32


def _attn_kernel(q_ref, k_ref, v_ref, o_ref, qm_ref, bias_ref, u_ref, xs_ref, z_ref, a_ref,
                 carry_ref, acc_ref):
    s_len = q_ref.shape[1]
    nsup = s_len // SUPER
    blk = ATT_BLOCK

    @pl.when((pl.program_id(0) == 0) & (pl.program_id(1) == 0))
    def _():
        drow = lax.broadcasted_iota(jnp.int32, (ITEM_ROWS, SUPER), 0)
        dcol = lax.broadcasted_iota(jnp.int32, (ITEM_ROWS, SUPER), 1)
        qpos = ((drow >> 8) << 7) + (drow & (blk - 1))
        bias_ref[...] = jnp.where(dcol < qpos, 0.0, MASKED).astype(F32)
        krow = lax.broadcasted_iota(jnp.int32, (SUPER, SUPER), 0)
        kcol = lax.broadcasted_iota(jnp.int32, (SUPER, SUPER), 1)
        u_ref[...] = jnp.where(krow >= kcol, -1.0, 0.0).astype(BF16)

    lane = lax.broadcasted_iota(jnp.int32, (blk, LANES), 1)
    first_head = lane < SB_HEAD_DIM
    for m in range(nsup):
        for half in range(2):
            qb = q_ref[0, (2 * m + half) * blk:(2 * m + half + 1) * blk, :]
            zero = jnp.zeros_like(qb)
            qm_ref[m, (2 * half) * blk:(2 * half + 1) * blk, :] = jnp.where(first_head, qb, zero)
            qm_ref[m, (2 * half + 1) * blk:(2 * half + 2) * blk, :] = jnp.where(first_head, zero, qb)
    def key_rows(c):
        return slice(c * SUPER, (c + 1) * SUPER)

    def scores(item, slot, hf):
        m, c = item
        per_half = SUPER // ROW_CHUNK
        zhalf = lax.dot_general(qm_ref[m, hf * SUPER:(hf + 1) * SUPER, :], k_ref[0, key_rows(c), :],
                                (((1,), (1,)), ((), ())), preferred_element_type=F32)
        for rc in range(hf * per_half, (hf + 1) * per_half):
            rows = slice(rc * ROW_CHUNK, (rc + 1) * ROW_CHUNK)
            zrows = slice((rc - hf * per_half) * ROW_CHUNK, (rc - hf * per_half + 1) * ROW_CHUNK)
            live = blk if (m == c and rc < SUPER // ROW_CHUNK) else SUPER
            z2 = zhalf[zrows, :live]
            if m == c:
                z2 = z2 + bias_ref[rows, :live]
            sp = jnp.maximum(z2, 0.0) + jnp.log2(1.0 + jnp.exp2(-jnp.abs(z2)))
            xs_ref[slot, rows, :live] = sp.astype(BF16)
            total = jnp.sum(sp, axis=-1, keepdims=True)
            if m == c:
                z_ref[slot, rows, :live] = z2
                carry_ref[m, rows, :] = jnp.broadcast_to(-total, (ROW_CHUNK, blk))
            else:
                carry = carry_ref[m, rows, :]
                z_ref[slot, rows, :live] = z2 + jnp.concatenate([carry, carry], axis=1)
                carry_ref[m, rows, :] = carry - total
            if live < SUPER:
                xs_ref[slot, rows, live:] = jnp.zeros((ROW_CHUNK, SUPER - live), BF16)
                z_ref[slot, rows, live:] = jnp.full((ROW_CHUNK, SUPER - live), MASKED, F32)

    def weights(slot_in, slot_out):
        st = jnp.dot(xs_ref[slot_in], u_ref[...], preferred_element_type=F32)
        for rc in range(ITEM_ROWS // ROW_CHUNK):
            rows = slice(rc * ROW_CHUNK, (rc + 1) * ROW_CHUNK)
            a_ref[slot_out, rows, :] = jnp.exp2(z_ref[slot_in, rows, :] + st[rows]).astype(BF16)

    def values(item, slot):
        m, c = item
        av = jnp.dot(a_ref[slot], v_ref[0, key_rows(c), :], preferred_element_type=F32)
        if m == c:
            acc_ref[m] = av
        else:
            acc_ref[m] += av

    items = [(m, c) for m in range(nsup) for c in range(m, -1, -1)]
    for t in range(len(items) + 2):
        cur, prev = t % 2, (t + 1) % 2
        if t < len(items):
            scores(items[t], cur, 0)
        if 0 <= t - 2 < len(items):
            values(items[t - 2], prev)
        if t < len(items):
            scores(items[t], cur, 1)
        if 0 <= t - 1 < len(items):
            weights(prev, cur)

    for m in range(nsup):
        for half in range(2):
            h0 = acc_ref[m, (2 * half) * blk:(2 * half + 1) * blk, :]
            h1 = acc_ref[m, (2 * half + 1) * blk:(2 * half + 2) * blk, :]
            o_ref[0, (2 * m + half) * blk:(2 * m + half + 1) * blk, :] = jnp.where(first_head, h0, h1)


def _attention(proj3d):
    b, s, _ = proj3d.shape
    nsup = s // SUPER
    nslot = 2
    spec = lambda off: pl.BlockSpec((1, s, LANES), lambda bi, p: (bi, 0, off + p))
    return pl.pallas_call(
        _attn_kernel,
        grid=(b, HEAD_PAIRS),
        in_specs=[spec(0), spec(HEAD_PAIRS), spec(2 * HEAD_PAIRS)],
        out_specs=pl.BlockSpec((1, s, LANES), lambda bi, p: (bi, 0, p)),
        out_shape=jax.ShapeDtypeStruct((b, s, SB_WIDTH), F32),
        scratch_shapes=[
            pltpu.VMEM((nsup, ITEM_ROWS, LANES), BF16),
            pltpu.VMEM((ITEM_ROWS, SUPER), F32),
            pltpu.VMEM((SUPER, SUPER), BF16),
            pltpu.VMEM((nslot, ITEM_ROWS, SUPER), BF16),
            pltpu.VMEM((nslot, ITEM_ROWS, SUPER), F32),
            pltpu.VMEM((nslot, ITEM_ROWS, SUPER), BF16),
            pltpu.VMEM((nsup, ITEM_ROWS, ATT_BLOCK), F32),
            pltpu.VMEM((nsup, ITEM_ROWS, LANES), F32),
        ],
        compiler_params=pltpu.CompilerParams(
            dimension_semantics=("arbitrary", "arbitrary"), vmem_limit_bytes=VMEM_LIMIT_BYTES),
        name="stickbreak_attn",
    )(proj3d, proj3d, proj3d)


def _mix_mlp_kernel(x_ref, a_ref, u_ref, vn_ref, sgw_ref, sgb_ref,
                    ong_ref, wout_ref, nmg_ref, wup_ref, wdn_ref, nfg_ref, o_ref,
                    mixed_ref):
    tm = x_ref.shape[0]
    vn = vn_ref[...]

    pos_t = lax.broadcasted_iota(jnp.int32, (SG_BLOCK, SG_BLOCK), 0)
    pos_s = lax.broadcasted_iota(jnp.int32, (SG_BLOCK, SG_BLOCK), 1)
    chunk_mask = (pos_s // CHUNK) <= (pos_t // CHUNK)
    lane = lax.broadcasted_iota(jnp.int32, (SG_BLOCK, LANES), 1)
    first = lane < SG_GROUP_DIM
    zero = jnp.zeros((SG_BLOCK, LANES), BF16)
    for pair in range(SG_GROUPS // 2):
        w0 = jnp.where(chunk_mask, sgw_ref[2 * pair], 0.0).astype(BF16)
        w1 = jnp.where(chunk_mask, sgw_ref[2 * pair + 1], 0.0).astype(BF16)
        w_pair = jnp.concatenate([w0, w1], axis=1)
        for r in range(tm // SG_BLOCK):
            vblk = vn[r * SG_BLOCK:(r + 1) * SG_BLOCK, pair * LANES:(pair + 1) * LANES]
            rhs = jnp.concatenate([jnp.where(first, vblk, zero), jnp.where(first, zero, vblk)], axis=0)
            mixed_ref[r * SG_BLOCK:(r + 1) * SG_BLOCK, pair * LANES:(pair + 1) * LANES] = (
                jnp.dot(w_pair, rhs, preferred_element_type=F32)
                + sgb_ref[:, pair * LANES:(pair + 1) * LANES])
    b_out = u_ref[...].astype(F32) * mixed_ref[...]

    ong = ong_ref[...]
    a_n = _rms(a_ref[...], ong[:, :SB_WIDTH]).astype(BF16)
    b_n = _rms(b_out, ong[:, SB_WIDTH:]).astype(BF16)
    mix = jnp.concatenate([a_n, b_n], axis=1)
    x1 = x_ref[...] + jnp.dot(mix, wout_ref[...], preferred_element_type=F32)

    h = _rms(x1, nmg_ref[...]).astype(BF16)
    y = x1
    fc = D_MODEL
    for c in range(D_FF // fc):
        up = jnp.dot(h, wup_ref[:, c * fc:(c + 1) * fc], preferred_element_type=F32)
        act = jnp.square(jnp.maximum(up, 0.0)).astype(BF16)
        y = y + jnp.dot(act, wdn_ref[c * fc:(c + 1) * fc, :], preferred_element_type=F32)
    o_ref[...] = _rms(y, nfg_ref[...])


def _mix_mlp(x2d, a2d, proj2d, sgw, sgb_full, ong, wout, nmg, wup, wdn, nfg, tm):
    n = x2d.shape[0]
    const = lambda shape: pl.BlockSpec(shape, lambda i: (0,) * len(shape),
                                       pipeline_mode=pl.Buffered(1))
    u_blk = 3 * SB_WIDTH // SG_WIDTH
    return pl.pallas_call(
        _mix_mlp_kernel,
        grid=(n // tm,),
        in_specs=[
            pl.BlockSpec((tm, D_MODEL), lambda i: (i, 0)),
            pl.BlockSpec((tm, SB_WIDTH), lambda i: (i, 0)),
            pl.BlockSpec((tm, SG_WIDTH), lambda i: (i, u_blk)),
            pl.BlockSpec((tm, SG_WIDTH), lambda i: (i, u_blk + 1)),
            const((SG_GROUPS, SG_BLOCK, SG_BLOCK)), const((SG_BLOCK, SG_WIDTH)),
            const((1, MIX_WIDTH)), const((MIX_WIDTH, D_MODEL)),
            const((1, D_MODEL)), const((D_MODEL, D_FF)), const((D_FF, D_MODEL)),
            const((1, D_MODEL)),
        ],
        out_specs=pl.BlockSpec((tm, D_MODEL), lambda i: (i, 0)),
        out_shape=jax.ShapeDtypeStruct((n, D_MODEL), F32),
        scratch_shapes=[pltpu.VMEM((tm, SG_WIDTH), F32)],
        compiler_params=pltpu.CompilerParams(
            dimension_semantics=("arbitrary",), vmem_limit_bytes=VMEM_LIMIT_BYTES),
        name="mix_mlp",
    )(x2d, a2d, proj2d, proj2d, sgw, sgb_full, ong, wout, nmg, wup, wdn, nfg)


def kernel(x, norm_mix_g, w_in, sg_ln_g, sg_ln_b, sg_w, sg_b, out_norm_g, w_out,
           norm_mlp_g, w_up, w_down, norm_final_g):
    b, s, d = x.shape
    n = b * s
    assert w_in.shape[0] == 1, "the final norm is fused into the (single) layer's last kernel"
    layer = 0
    x2d = x.reshape(n, d)
    proj = _inproj(x2d, norm_mix_g[layer][None], w_in[layer].astype(BF16),
                   sg_ln_g[layer][None], sg_ln_b[layer][None], tm=512)
    a_out = _attention(proj.reshape(b, s, IN_WIDTH))
    sgb_full = jnp.repeat(jnp.transpose(sg_b[layer]), SG_GROUP_DIM, axis=1)
    out = _mix_mlp(
        x2d, a_out.reshape(n, SB_WIDTH), proj, sg_w[layer], sgb_full,
        out_norm_g[layer][None], w_out[layer].astype(BF16), norm_mlp_g[layer][None],
        w_up[layer].astype(BF16), w_down[layer].astype(BF16), norm_final_g[None], tm=512)
    return out.reshape(b, s, d)
```

```python
import jax
import jax.numpy as jnp
from jax import lax
from jax.experimental import pallas as pl
from jax.experimental.pallas import tpu as pltpu

D_MODEL = 1024
CHUNK = 64
SB_HEADS = 8
SB_HEAD_DIM = 64
SB_WIDTH = SB_HEADS * SB_HEAD_DIM
SG_GROUPS = 8
SG_GROUP_DIM = 64
SG_WIDTH = SG_GROUPS * SG_GROUP_DIM
SG_BLOCK = 128
MIX_WIDTH = SB_WIDTH + SG_WIDTH
IN_WIDTH = 3 * SB_WIDTH + 2 * SG_WIDTH
D_FF = 4 * D_MODEL
EPS = 1e-6

LANES = 128
ATT_BLOCK = 128
SUPER = 2 * ATT_BLOCK
ITEM_ROWS = 2 * SUPER
HEAD_PAIRS = SB_HEADS * SB_HEAD_DIM // LANES
Q_SCALE = 1.4426950408889634 / (SB_HEAD_DIM ** 0.5)
VMEM_LIMIT_BYTES = 56 * 1024 * 1024

F32 = jnp.float32
BF16 = jnp.bfloat16


def _rms(x, g):
    ms = jnp.mean(x * x, axis=-1, keepdims=True)
    return x * lax.rsqrt(ms + EPS) * g


def _cast_once(src_ref, dst_ref, col_block):
    @pl.when(pl.program_id(0) == 0)
    def _():
        for lo in range(0, src_ref.shape[1], col_block):
            dst_ref[:, lo:lo + col_block] = src_ref[:, lo:lo + col_block].astype(dst_ref.dtype)


def _inproj_kernel(x_ref, g_ref, wf32_ref, lng_ref, lnb_ref, o_ref, w_ref):
    _cast_once(wf32_ref, w_ref, 2 * LANES)
    qkv = 3 * SB_WIDTH
    half = SG_WIDTH // 2
    h = _rms(x_ref[...], g_ref[...]).astype(BF16)

    def cols(lo, width):
        return jnp.dot(h, w_ref[:, lo:lo + width], preferred_element_type=F32)

    vacts = []
    for step, part in enumerate((2, 3, 0, 1)):
        lo = qkv + part * half
        act = jax.nn.gelu(cols(lo, half))
        if part < 2:
            o_ref[:, lo:lo + half] = act.astype(o_ref.dtype)
        else:
            vacts.append(act)
        if part == 3:
            vact = jnp.concatenate(vacts, axis=1)
            mu = jnp.mean(vact, axis=-1, keepdims=True)
            vc = vact - mu
            var = jnp.mean(vc * vc, axis=-1, keepdims=True)
            vn = vc * lax.rsqrt(var + EPS) * lng_ref[...] + lnb_ref[...]
            o_ref[:, qkv + SG_WIDTH:] = vn.astype(o_ref.dtype)
        if step < 3:
            p = cols(step * SB_WIDTH, SB_WIDTH)
            if step == 0:
                p = p * Q_SCALE
            o_ref[:, step * SB_WIDTH:(step + 1) * SB_WIDTH] = p.astype(o_ref.dtype)


def _inproj(x2d, g, w_f32, lng, lnb, tm):
    n = x2d.shape[0]
    const = lambda shape: pl.BlockSpec(shape, lambda i: (0, 0), pipeline_mode=pl.Buffered(1))
    return pl.pallas_call(
        _inproj_kernel,
        grid=(n // tm,),
        in_specs=[
            pl.BlockSpec((tm, D_MODEL), lambda i: (i, 0)),
            const((1, D_MODEL)), const((D_MODEL, IN_WIDTH)),
            const((1, SG_WIDTH)), const((1, SG_WIDTH)),
        ],
        out_specs=pl.BlockSpec((tm, IN_WIDTH), lambda i: (i, 0)),
        out_shape=jax.ShapeDtypeStruct((n, IN_WIDTH), BF16),
        scratch_shapes=[pltpu.VMEM((D_MODEL, IN_WIDTH), BF16)],
        compiler_params=pltpu.CompilerParams(
            dimension_semantics=("arbitrary",), vmem_limit_bytes=VMEM_LIMIT_BYTES),
        name="inproj",
    )(x2d, g, w_f32, lng, lnb)


MASKED = -1e30
ROW_CHUNK = 32


def _attn_kernel(q_ref, k_ref, v_ref, o_ref, qm_ref, bias_ref, u_ref, xs_ref, z_ref, a_ref,
                 carry_ref, acc_ref):
    s_len = q_ref.shape[1]
    nsup = s_len // SUPER
    blk = ATT_BLOCK

    @pl.when((pl.program_id(0) == 0) & (pl.program_id(1) == 0))
    def _():
        drow = lax.broadcasted_iota(jnp.int32, (ITEM_ROWS, SUPER), 0)
        dcol = lax.broadcasted_iota(jnp.int32, (ITEM_ROWS, SUPER), 1)
        qpos = ((drow >> 8) << 7) + (drow & (blk - 1))
        bias_ref[...] = jnp.where(dcol < qpos, 0.0, MASKED).astype(F32)
        krow = lax.broadcasted_iota(jnp.int32, (SUPER, SUPER), 0)
        kcol = lax.broadcasted_iota(jnp.int32, (SUPER, SUPER), 1)
        u_ref[...] = jnp.where(krow >= kcol, -1.0, 0.0).astype(BF16)

    lane = lax.broadcasted_iota(jnp.int32, (blk, LANES), 1)
    first_head = lane < SB_HEAD_DIM
    for m in range(nsup):
        for half in range(2):
            qb = q_ref[0, (2 * m + half) * blk:(2 * m + half + 1) * blk, :]
            zero = jnp.zeros_like(qb)
            qm_ref[m, (2 * half) * blk:(2 * half + 1) * blk, :] = jnp.where(first_head, qb, zero)
            qm_ref[m, (2 * half + 1) * blk:(2 * half + 2) * blk, :] = jnp.where(first_head, zero, qb)
    def key_rows(c):
        return slice(c * SUPER, (c + 1) * SUPER)

    def scores(item, slot, hf):
        m, c = item
        per_half = SUPER // ROW_CHUNK
        zhalf = lax.dot_general(qm_ref[m, hf * SUPER:(hf + 1) * SUPER, :], k_ref[0, key_rows(c), :],
                                (((1,), (1,)), ((), ())), preferred_element_type=F32)
        for rc in range(hf * per_half, (hf + 1) * per_half):
            rows = slice(rc * ROW_CHUNK, (rc + 1) * ROW_CHUNK)
            zrows = slice((rc - hf * per_half) * ROW_CHUNK, (rc - hf * per_half + 1) * ROW_CHUNK)
            live = blk if (m == c and rc < SUPER // ROW_CHUNK) else SUPER
            z2 = zhalf[zrows, :live]
            if m == c:
                z2 = z2 + bias_ref[rows, :live]
            sp = jnp.maximum(z2, 0.0) + jnp.log2(1.0 + jnp.exp2(-jnp.abs(z2)))
            xs_ref[slot, rows, :live] = sp.astype(BF16)
            total = jnp.sum(sp, axis=-1, keepdims=True)
            if m == c:
                z_ref[slot, rows, :live] = z2
                carry_ref[m, rows, :] = jnp.broadcast_to(-total, (ROW_CHUNK, blk))
            else:
                carry = carry_ref[m, rows, :]
                z_ref[slot, rows, :live] = z2 + jnp.concatenate([carry, carry], axis=1)
                carry_ref[m, rows, :] = carry - total
            if live < SUPER:
                xs_ref[slot, rows, live:] = jnp.zeros((ROW_CHUNK, SUPER - live), BF16)
                z_ref[slot, rows, live:] = jnp.full((ROW_CHUNK, SUPER - live), MASKED, F32)

    def weights(slot_in, slot_out):
        st = jnp.dot(xs_ref[slot_in], u_ref[...], preferred_element_type=F32)
        for rc in range(ITEM_ROWS // ROW_CHUNK):
            rows = slice(rc * ROW_CHUNK, (rc + 1) * ROW_CHUNK)
            a_ref[slot_out, rows, :] = jnp.exp2(z_ref[slot_in, rows, :] + st[rows]).astype(BF16)

    def values(item, slot):
        m, c = item
        av = jnp.dot(a_ref[slot], v_ref[0, key_rows(c), :], preferred_element_type=F32)
        if m == c:
            acc_ref[m] = av
        else:
            acc_ref[m] += av

    items = [(m, c) for m in range(nsup) for c in range(m, -1, -1)]
    for t in range(len(items) + 2):
        cur, prev = t % 2, (t + 1) % 2
        if t < len(items):
            scores(items[t], cur, 0)
        if 0 <= t - 2 < len(items):
            values(items[t - 2], prev)
        if t < len(items):
            scores(items[t], cur, 1)
        if 0 <= t - 1 < len(items):
            weights(prev, cur)

    for m in range(nsup):
        for half in range(2):
            h0 = acc_ref[m, (2 * half) * blk:(2 * half + 1) * blk, :]
            h1 = acc_ref[m, (2 * half + 1) * blk:(2 * half + 2) * blk, :]
            o_ref[0, (2 * m + half) * blk:(2 * m + half + 1) * blk, :] = jnp.where(first_head, h0, h1)


def _attention(proj3d):
    b, s, _ = proj3d.shape
    nsup = s // SUPER
    nslot = 2
    spec = lambda off: pl.BlockSpec((1, s, LANES), lambda bi, p: (bi, 0, off + p))
    return pl.pallas_call(
        _attn_kernel,
        grid=(b, HEAD_PAIRS),
        in_specs=[spec(0), spec(HEAD_PAIRS), spec(2 * HEAD_PAIRS)],
        out_specs=pl.BlockSpec((1, s, LANES), lambda bi, p: (bi, 0, p)),
        out_shape=jax.ShapeDtypeStruct((b, s, SB_WIDTH), F32),
        scratch_shapes=[
            pltpu.VMEM((nsup, ITEM_ROWS, LANES), BF16),
            pltpu.VMEM((ITEM_ROWS, SUPER), F32),
            pltpu.VMEM((SUPER, SUPER), BF16),
            pltpu.VMEM((nslot, ITEM_ROWS, SUPER), BF16),
            pltpu.VMEM((nslot, ITEM_ROWS, SUPER), F32),
            pltpu.VMEM((nslot, ITEM_ROWS, SUPER), BF16),
            pltpu.VMEM((nsup, ITEM_ROWS, ATT_BLOCK), F32),
            pltpu.VMEM((nsup, ITEM_ROWS, LANES), F32),
        ],
        compiler_params=pltpu.CompilerParams(
            dimension_semantics=("arbitrary", "arbitrary"), vmem_limit_bytes=VMEM_LIMIT_BYTES),
        name="stickbreak_attn",
    )(proj3d, proj3d, proj3d)


def _mix_mlp_kernel(x_ref, a_ref, u_ref, vn_ref, sgw_ref, sgb_ref,
                    ong_ref, woutf32_ref, nmg_ref, wup_ref, wdn_ref, nfg_ref, o_ref,
                    mixed_ref, wout_ref):
    _cast_once(woutf32_ref, wout_ref, 2 * LANES)
    tm = x_ref.shape[0]
    vn = vn_ref[...]

    pos_t = lax.broadcasted_iota(jnp.int32, (SG_BLOCK, SG_BLOCK), 0)
    pos_s = lax.broadcasted_iota(jnp.int32, (SG_BLOCK, SG_BLOCK), 1)
    chunk_mask = (pos_s // CHUNK) <= (pos_t // CHUNK)
    lane = lax.broadcasted_iota(jnp.int32, (SG_BLOCK, LANES), 1)
    first = lane < SG_GROUP_DIM
    zero = jnp.zeros((SG_BLOCK, LANES), BF16)
    for pair in range(SG_GROUPS // 2):
        w0 = jnp.where(chunk_mask, sgw_ref[2 * pair], 0.0).astype(BF16)
        w1 = jnp.where(chunk_mask, sgw_ref[2 * pair + 1], 0.0).astype(BF16)
        w_pair = jnp.concatenate([w0, w1], axis=1)
        for r in range(tm // SG_BLOCK):
            vblk = vn[r * SG_BLOCK:(r + 1) * SG_BLOCK, pair * LANES:(pair + 1) * LANES]
            rhs = jnp.concatenate([jnp.where(first, vblk, zero), jnp.where(first, zero, vblk)], axis=0)
            mixed_ref[r * SG_BLOCK:(r + 1) * SG_BLOCK, pair * LANES:(pair + 1) * LANES] = (
                jnp.dot(w_pair, rhs, preferred_element_type=F32)
                + sgb_ref[:, pair * LANES:(pair + 1) * LANES])
    b_out = u_ref[...].astype(F32) * mixed_ref[...]

    ong = ong_ref[...]
    a_n = _rms(a_ref[...], ong[:, :SB_WIDTH]).astype(BF16)
    b_n = _rms(b_out, ong[:, SB_WIDTH:]).astype(BF16)
    mix = jnp.concatenate([a_n, b_n], axis=1)
    x1 = x_ref[...] + jnp.dot(mix, wout_ref[...], preferred_element_type=F32)

    h = _rms(x1, nmg_ref[...]).astype(BF16)
    y = x1
    fc = D_MODEL
    for c in range(D_FF // fc):
        up = jnp.dot(h, wup_ref[:, c * fc:(c + 1) * fc], preferred_element_type=F32)
        act = jnp.square(jnp.maximum(up, 0.0)).astype(BF16)
        y = y + jnp.dot(act, wdn_ref[c * fc:(c + 1) * fc, :], preferred_element_type=F32)
    o_ref[...] = _rms(y, nfg_ref[...])


def _mix_mlp(x2d, a2d, proj2d, sgw, sgb_full, ong, wout, nmg, wup, wdn, nfg, tm):
    n = x2d.shape[0]
    const = lambda shape: pl.BlockSpec(shape, lambda i: (0,) * len(shape),
                                       pipeline_mode=pl.Buffered(1))
    u_blk = 3 * SB_WIDTH // SG_WIDTH
    return pl.pallas_call(
        _mix_mlp_kernel,
        grid=(n // tm,),
        in_specs=[
            pl.BlockSpec((tm, D_MODEL), lambda i: (i, 0)),
            pl.BlockSpec((tm, SB_WIDTH), lambda i: (i, 0)),
            pl.BlockSpec((tm, SG_WIDTH), lambda i: (i, u_blk)),
            pl.BlockSpec((tm, SG_WIDTH), lambda i: (i, u_blk + 1)),
            const((SG_GROUPS, SG_BLOCK, SG_BLOCK)), const((SG_BLOCK, SG_WIDTH)),
            const((1, MIX_WIDTH)), const((MIX_WIDTH, D_MODEL)),
            const((1, D_MODEL)), const((D_MODEL, D_FF)), const((D_FF, D_MODEL)),
            const((1, D_MODEL)),
        ],
        out_specs=pl.BlockSpec((tm, D_MODEL), lambda i: (i, 0)),
        out_shape=jax.ShapeDtypeStruct((n, D_MODEL), F32),
        scratch_shapes=[pltpu.VMEM((tm, SG_WIDTH), F32), pltpu.VMEM((MIX_WIDTH, D_MODEL), BF16)],
        compiler_params=pltpu.CompilerParams(
            dimension_semantics=("arbitrary",), vmem_limit_bytes=VMEM_LIMIT_BYTES),
        name="mix_mlp",
    )(x2d, a2d, proj2d, proj2d, sgw, sgb_full, ong, wout, nmg, wup, wdn, nfg)


def kernel(x, norm_mix_g, w_in, sg_ln_g, sg_ln_b, sg_w, sg_b, out_norm_g, w_out,
           norm_mlp_g, w_up, w_down, norm_final_g):
    b, s, d = x.shape
    n = b * s
    assert w_in.shape[0] == 1, "the final norm is fused into the (single) layer's last kernel"
    layer = 0
    x2d = x.reshape(n, d)
    proj = _inproj(x2d, norm_mix_g[layer][None], w_in[layer],
                   sg_ln_g[layer][None], sg_ln_b[layer][None], tm=512)
    a_out = _attention(proj.reshape(b, s, IN_WIDTH))
    sgb_full = jnp.repeat(jnp.transpose(sg_b[layer]), SG_GROUP_DIM, axis=1)
    out = _mix_mlp(
        x2d, a_out.reshape(n, SB_WIDTH), proj, sg_w[layer], sgb_full,
        out_norm_g[layer][None], w_out[layer], norm_mlp_g[layer][None],
        w_up[layer].astype(BF16), w_down[layer].astype(BF16), norm_final_g[None], tm=512)
    return out.reshape(b, s, d)
```

```python
import jax
import jax.numpy as jnp
from jax import lax
from jax.experimental import pallas as pl
from jax.experimental.pallas import tpu as pltpu

D_MODEL = 1024
CHUNK = 64
SB_HEADS = 8
SB_HEAD_DIM = 64
SB_WIDTH = SB_HEADS * SB_HEAD_DIM
SG_GROUPS = 8
SG_GROUP_DIM = 64
SG_WIDTH = SG_GROUPS * SG_GROUP_DIM
SG_BLOCK = 128
MIX_WIDTH = SB_WIDTH + SG_WIDTH
IN_WIDTH = 3 * SB_WIDTH + 2 * SG_WIDTH
D_FF = 4 * D_MODEL
EPS = 1e-6

LANES = 128
ATT_BLOCK = 128
SUPER = 2 * ATT_BLOCK
ITEM_ROWS = 2 * SUPER
HEAD_PAIRS = SB_HEADS * SB_HEAD_DIM // LANES
Q_SCALE = 1.4426950408889634 / (SB_HEAD_DIM ** 0.5)
VMEM_LIMIT_BYTES = 56 * 1024 * 1024

F32 = jnp.float32
BF16 = jnp.bfloat16


def _rms(x, g):
    ms = jnp.mean(x * x, axis=-1, keepdims=True)
    return x * lax.rsqrt(ms + EPS) * g


def _cast_once(src_ref, dst_ref, col_block):
    @pl.when(pl.program_id(0) == 0)
    def _():
        for lo in range(0, src_ref.shape[1], col_block):
            dst_ref[:, lo:lo + col_block] = src_ref[:, lo:lo + col_block].astype(dst_ref.dtype)


def _inproj_kernel(x_ref, g_ref, wf32_ref, lng_ref, lnb_ref, o_ref, w_ref):
    _cast_once(wf32_ref, w_ref, 2 * LANES)
    qkv = 3 * SB_WIDTH
    half = SG_WIDTH // 2
    h = _rms(x_ref[...], g_ref[...]).astype(BF16)

    def cols(lo, width):
        return jnp.dot(h, w_ref[:, lo:lo + width], preferred_element_type=F32)

    vacts = []
    for step, part in enumerate((2, 3, 0, 1)):
        lo = qkv + part * half
        act = jax.nn.gelu(cols(lo, half))
        if part < 2:
            o_ref[:, lo:lo + half] = act.astype(o_ref.dtype)
        else:
            vacts.append(act)
        if part == 3:
            vact = jnp.concatenate(vacts, axis=1)
            mu = jnp.mean(vact, axis=-1, keepdims=True)
            vc = vact - mu
            var = jnp.mean(vc * vc, axis=-1, keepdims=True)
            vn = vc * lax.rsqrt(var + EPS) * lng_ref[...] + lnb_ref[...]
            o_ref[:, qkv + SG_WIDTH:] = vn.astype(o_ref.dtype)
        if step < 3:
            p = cols(step * SB_WIDTH, SB_WIDTH)
            if step == 0:
                p = p * Q_SCALE
            o_ref[:, step * SB_WIDTH:(step + 1) * SB_WIDTH] = p.astype(o_ref.dtype)


def _inproj(x2d, g, w_f32, lng, lnb, tm):
    n = x2d.shape[0]
    const = lambda shape: pl.BlockSpec(shape, lambda i: (0, 0), pipeline_mode=pl.Buffered(1))
    return pl.pallas_call(
        _inproj_kernel,
        grid=(n // tm,),
        in_specs=[
            pl.BlockSpec((tm, D_MODEL), lambda i: (i, 0)),
            const((1, D_MODEL)), const((D_MODEL, IN_WIDTH)),
            const((1, SG_WIDTH)), const((1, SG_WIDTH)),
        ],
        out_specs=pl.BlockSpec((tm, IN_WIDTH), lambda i: (i, 0)),
        out_shape=jax.ShapeDtypeStruct((n, IN_WIDTH), BF16),
        scratch_shapes=[pltpu.VMEM((D_MODEL, IN_WIDTH), BF16)],
        compiler_params=pltpu.CompilerParams(
            dimension_semantics=("arbitrary",), vmem_limit_bytes=VMEM_LIMIT_BYTES),
        name="inproj",
    )(x2d, g, w_f32, lng, lnb)


MASKED = -1e30
ROW_CHUNK = 32


def _attn_kernel(q_ref, k_ref, v_ref, wup_ref, wdn_ref, o_ref, wup_bf_ref, wdn_bf_ref,
                 qm_ref, bias_ref, u_ref, xs_ref, z_ref, a_ref, carry_ref, acc_ref):
    wup_bf_ref[...] = wup_ref[...].astype(BF16)
    wdn_bf_ref[...] = wdn_ref[...].astype(BF16)
    s_len = q_ref.shape[1]
    nsup = s_len // SUPER
    blk = ATT_BLOCK

    @pl.when((pl.program_id(0) == 0) & (pl.program_id(1) == 0))
    def _():
        drow = lax.broadcasted_iota(jnp.int32, (ITEM_ROWS, SUPER), 0)
        dcol = lax.broadcasted_iota(jnp.int32, (ITEM_ROWS, SUPER), 1)
        qpos = ((drow >> 8) << 7) + (drow & (blk - 1))
        bias_ref[...] = jnp.where(dcol < qpos, 0.0, MASKED).astype(F32)
        krow = lax.broadcasted_iota(jnp.int32, (SUPER, SUPER), 0)
        kcol = lax.broadcasted_iota(jnp.int32, (SUPER, SUPER), 1)
        u_ref[...] = jnp.where(krow >= kcol, -1.0, 0.0).astype(BF16)

    lane = lax.broadcasted_iota(jnp.int32, (blk, LANES), 1)
    first_head = lane < SB_HEAD_DIM
    for m in range(nsup):
        for half in range(2):
            qb = q_ref[0, (2 * m + half) * blk:(2 * m + half + 1) * blk, :]
            zero = jnp.zeros_like(qb)
            qm_ref[m, (2 * half) * blk:(2 * half + 1) * blk, :] = jnp.where(first_head, qb, zero)
            qm_ref[m, (2 * half + 1) * blk:(2 * half + 2) * blk, :] = jnp.where(first_head, zero, qb)
    def key_rows(c):
        return slice(c * SUPER, (c + 1) * SUPER)

    def scores(item, slot, hf):
        m, c = item
        per_half = SUPER // ROW_CHUNK
        zhalf = lax.dot_general(qm_ref[m, hf * SUPER:(hf + 1) * SUPER, :], k_ref[0, key_rows(c), :],
                                (((1,), (1,)), ((), ())), preferred_element_type=F32)
        for rc in range(hf * per_half, (hf + 1) * per_half):
            rows = slice(rc * ROW_CHUNK, (rc + 1) * ROW_CHUNK)
            zrows = slice((rc - hf * per_half) * ROW_CHUNK, (rc - hf * per_half + 1) * ROW_CHUNK)
            live = blk if (m == c and rc < SUPER // ROW_CHUNK) else SUPER
            z2 = zhalf[zrows, :live]
            if m == c:
                z2 = z2 + bias_ref[rows, :live]
            sp = jnp.maximum(z2, 0.0) + jnp.log2(1.0 + jnp.exp2(-jnp.abs(z2)))
            xs_ref[slot, rows, :live] = sp.astype(BF16)
            total = jnp.sum(sp, axis=-1, keepdims=True)
            if m == c:
                z_ref[slot, rows, :live] = z2
                carry_ref[m, rows, :] = jnp.broadcast_to(-total, (ROW_CHUNK, blk))
            else:
                carry = carry_ref[m, rows, :]
                z_ref[slot, rows, :live] = z2 + jnp.concatenate([carry, carry], axis=1)
                carry_ref[m, rows, :] = carry - total
            if live < SUPER:
                xs_ref[slot, rows, live:] = jnp.zeros((ROW_CHUNK, SUPER - live), BF16)
                z_ref[slot, rows, live:] = jnp.full((ROW_CHUNK, SUPER - live), MASKED, F32)

    def weights(slot_in, slot_out):
        st = jnp.dot(xs_ref[slot_in], u_ref[...], preferred_element_type=F32)
        for rc in range(ITEM_ROWS // ROW_CHUNK):
            rows = slice(rc * ROW_CHUNK, (rc + 1) * ROW_CHUNK)
            a_ref[slot_out, rows, :] = jnp.exp2(z_ref[slot_in, rows, :] + st[rows]).astype(BF16)

    def values(item, slot):
        m, c = item
        av = jnp.dot(a_ref[slot], v_ref[0, key_rows(c), :], preferred_element_type=F32)
        if m == c:
            acc_ref[m] = av
        else:
            acc_ref[m] += av

    items = [(m, c) for m in range(nsup) for c in range(m, -1, -1)]
    for t in range(len(items) + 2):
        cur, prev = t % 2, (t + 1) % 2
        if t < len(items):
            scores(items[t], cur, 0)
        if 0 <= t - 2 < len(items):
            values(items[t - 2], prev)
        if t < len(items):
            scores(items[t], cur, 1)
        if 0 <= t - 1 < len(items):
            weights(prev, cur)

    for m in range(nsup):
        for half in range(2):
            h0 = acc_ref[m, (2 * half) * blk:(2 * half + 1) * blk, :]
            h1 = acc_ref[m, (2 * half + 1) * blk:(2 * half + 2) * blk, :]
            o_ref[0, (2 * m + half) * blk:(2 * m + half + 1) * blk, :] = jnp.where(first_head, h0, h1)


def _attention(proj3d, w_up, w_down):
    b, s, _ = proj3d.shape
    nsup = s // SUPER
    nslot = 2
    steps = b * HEAD_PAIRS
    up_rows, dn_rows = w_up.shape[0] // steps, w_down.shape[0] // steps
    assert up_rows * steps == w_up.shape[0] and dn_rows * steps == w_down.shape[0]
    spec = lambda off: pl.BlockSpec((1, s, LANES), lambda bi, p: (bi, 0, off + p))
    slab = lambda rows, cols: pl.BlockSpec((rows, cols), lambda bi, p: (bi * HEAD_PAIRS + p, 0))
    return pl.pallas_call(
        _attn_kernel,
        grid=(b, HEAD_PAIRS),
        in_specs=[spec(0), spec(HEAD_PAIRS), spec(2 * HEAD_PAIRS),
                  slab(up_rows, w_up.shape[1]), slab(dn_rows, w_down.shape[1])],
        out_specs=[pl.BlockSpec((1, s, LANES), lambda bi, p: (bi, 0, p)),
                   slab(up_rows, w_up.shape[1]), slab(dn_rows, w_down.shape[1])],
        out_shape=[jax.ShapeDtypeStruct((b, s, SB_WIDTH), F32),
                   jax.ShapeDtypeStruct(w_up.shape, BF16),
                   jax.ShapeDtypeStruct(w_down.shape, BF16)],
        scratch_shapes=[
            pltpu.VMEM((nsup, ITEM_ROWS, LANES), BF16),
            pltpu.VMEM((ITEM_ROWS, SUPER), F32),
            pltpu.VMEM((SUPER, SUPER), BF16),
            pltpu.VMEM((nslot, ITEM_ROWS, SUPER), BF16),
            pltpu.VMEM((nslot, ITEM_ROWS, SUPER), F32),
            pltpu.VMEM((nslot, ITEM_ROWS, SUPER), BF16),
            pltpu.VMEM((nsup, ITEM_ROWS, ATT_BLOCK), F32),
            pltpu.VMEM((nsup, ITEM_ROWS, LANES), F32),
        ],
        compiler_params=pltpu.CompilerParams(
            dimension_semantics=("arbitrary", "arbitrary"), vmem_limit_bytes=VMEM_LIMIT_BYTES),
        name="stickbreak_attn",
    )(proj3d, proj3d, proj3d, w_up, w_down)


def _mix_mlp_kernel(x_ref, a_ref, u_ref, vn_ref, sgw_ref, sgb_ref,
                    ong_ref, woutf32_ref, nmg_ref, wup_ref, wdn_ref, nfg_ref, o_ref,
                    mixed_ref, wout_ref):
    _cast_once(woutf32_ref, wout_ref, 2 * LANES)
    tm = x_ref.shape[0]
    vn = vn_ref[...]

    pos_t = lax.broadcasted_iota(jnp.int32, (SG_BLOCK, SG_BLOCK), 0)
    pos_s = lax.broadcasted_iota(jnp.int32, (SG_BLOCK, SG_BLOCK), 1)
    chunk_mask = (pos_s // CHUNK) <= (pos_t // CHUNK)
    lane = lax.broadcasted_iota(jnp.int32, (SG_BLOCK, LANES), 1)
    first = lane < SG_GROUP_DIM
    zero = jnp.zeros((SG_BLOCK, LANES), BF16)
    for pair in range(SG_GROUPS // 2):
        w0 = jnp.where(chunk_mask, sgw_ref[2 * pair], 0.0).astype(BF16)
        w1 = jnp.where(chunk_mask, sgw_ref[2 * pair + 1], 0.0).astype(BF16)
        w_pair = jnp.concatenate([w0, w1], axis=1)
        for r in range(tm // SG_BLOCK):
            vblk = vn[r * SG_BLOCK:(r + 1) * SG_BLOCK, pair * LANES:(pair + 1) * LANES]
            rhs = jnp.concatenate([jnp.where(first, vblk, zero), jnp.where(first, zero, vblk)], axis=0)
            mixed_ref[r * SG_BLOCK:(r + 1) * SG_BLOCK, pair * LANES:(pair + 1) * LANES] = (
                jnp.dot(w_pair, rhs, preferred_element_type=F32)
                + sgb_ref[:, pair * LANES:(pair + 1) * LANES])
    b_out = u_ref[...].astype(F32) * mixed_ref[...]

    ong = ong_ref[...]
    a_n = _rms(a_ref[...], ong[:, :SB_WIDTH]).astype(BF16)
    b_n = _rms(b_out, ong[:, SB_WIDTH:]).astype(BF16)
    mix = jnp.concatenate([a_n, b_n], axis=1)
    x1 = x_ref[...] + jnp.dot(mix, wout_ref[...], preferred_element_type=F32)

    h = _rms(x1, nmg_ref[...]).astype(BF16)
    y = x1
    fc = D_MODEL
    for c in range(D_FF // fc):
        up = jnp.dot(h, wup_ref[:, c * fc:(c + 1) * fc], preferred_element_type=F32)
        act = jnp.square(jnp.maximum(up, 0.0)).astype(BF16)
        y = y + jnp.dot(act, wdn_ref[c * fc:(c + 1) * fc, :], preferred_element_type=F32)
    o_ref[...] = _rms(y, nfg_ref[...])


def _mix_mlp(x2d, a2d, proj2d, sgw, sgb_full, ong, wout, nmg, wup, wdn, nfg, tm):
    n = x2d.shape[0]
    const = lambda shape: pl.BlockSpec(shape, lambda i: (0,) * len(shape),
                                       pipeline_mode=pl.Buffered(1))
    u_blk = 3 * SB_WIDTH // SG_WIDTH
    return pl.pallas_call(
        _mix_mlp_kernel,
        grid=(n // tm,),
        in_specs=[
            pl.BlockSpec((tm, D_MODEL), lambda i: (i, 0)),
            pl.BlockSpec((tm, SB_WIDTH), lambda i: (i, 0)),
            pl.BlockSpec((tm, SG_WIDTH), lambda i: (i, u_blk)),
            pl.BlockSpec((tm, SG_WIDTH), lambda i: (i, u_blk + 1)),
            const((SG_GROUPS, SG_BLOCK, SG_BLOCK)), const((SG_BLOCK, SG_WIDTH)),
            const((1, MIX_WIDTH)), const((MIX_WIDTH, D_MODEL)),
            const((1, D_MODEL)), const((D_MODEL, D_FF)), const((D_FF, D_MODEL)),
            const((1, D_MODEL)),
        ],
        out_specs=pl.BlockSpec((tm, D_MODEL), lambda i: (i, 0)),
        out_shape=jax.ShapeDtypeStruct((n, D_MODEL), F32),
        scratch_shapes=[pltpu.VMEM((tm, SG_WIDTH), F32), pltpu.VMEM((MIX_WIDTH, D_MODEL), BF16)],
        compiler_params=pltpu.CompilerParams(
            dimension_semantics=("arbitrary",), vmem_limit_bytes=VMEM_LIMIT_BYTES),
        name="mix_mlp",
    )(x2d, a2d, proj2d, proj2d, sgw, sgb_full, ong, wout, nmg, wup, wdn, nfg)


def kernel(x, norm_mix_g, w_in, sg_ln_g, sg_ln_b, sg_w, sg_b, out_norm_g, w_out,
           norm_mlp_g, w_up, w_down, norm_final_g):
    b, s, d = x.shape
    n = b * s
    assert w_in.shape[0] == 1, "the final norm is fused into the (single) layer's last kernel"
    layer = 0
    x2d = x.reshape(n, d)
    proj = _inproj(x2d, norm_mix_g[layer][None], w_in[layer],
                   sg_ln_g[layer][None], sg_ln_b[layer][None], tm=512)
    a_out, w_up_bf, w_down_bf = _attention(proj.reshape(b, s, IN_WIDTH), w_up[layer], w_down[layer])
    sgb_full = jnp.repeat(jnp.transpose(sg_b[layer]), SG_GROUP_DIM, axis=1)
    out = _mix_mlp(
        x2d, a_out.reshape(n, SB_WIDTH), proj, sg_w[layer], sgb_full,
        out_norm_g[layer][None], w_out[layer], norm_mlp_g[layer][None],
        w_up_bf, w_down_bf, norm_final_g[None], tm=512)
    return out.reshape(b, s, d)
```

```python
import jax
import jax.numpy as jnp
from jax import lax
from jax.experimental import pallas as pl
from jax.experimental.pallas import tpu as pltpu

D_MODEL = 1024
CHUNK = 64
SB_HEADS = 8
SB_HEAD_DIM = 64
SB_WIDTH = SB_HEADS * SB_HEAD_DIM
SG_GROUPS = 8
SG_GROUP_DIM = 64
SG_WIDTH = SG_GROUPS * SG_GROUP_DIM
SG_BLOCK = 128
MIX_WIDTH = SB_WIDTH + SG_WIDTH
IN_WIDTH = 3 * SB_WIDTH + 2 * SG_WIDTH
D_FF = 4 * D_MODEL
EPS = 1e-6

LANES = 128
ATT_BLOCK = 128
SUPER = 2 * ATT_BLOCK
ITEM_ROWS = 2 * SUPER
HEAD_PAIRS = SB_HEADS * SB_HEAD_DIM // LANES
Q_SCALE = 1.4426950408889634 / (SB_HEAD_DIM ** 0.5)
VMEM_LIMIT_BYTES = 56 * 1024 * 1024

F32 = jnp.float32
BF16 = jnp.bfloat16


def _rms(x, g):
    ms = jnp.mean(x * x, axis=-1, keepdims=True)
    return x * lax.rsqrt(ms + EPS) * g


def _cast_once(src_ref, dst_ref, col_block):
    @pl.when(pl.program_id(0) == 0)
    def _():
        for lo in range(0, src_ref.shape[1], col_block):
            dst_ref[:, lo:lo + col_block] = src_ref[:, lo:lo + col_block].astype(dst_ref.dtype)


def _inproj_kernel(x_ref, g_ref, wf32_ref, lng_ref, lnb_ref, wup_ref, wdn_ref, o_ref, wup_bf_ref, wdn_bf_ref,
                   w_ref):
    _cast_once(wf32_ref, w_ref, 2 * LANES)
    wup_bf_ref[...] = wup_ref[...].astype(BF16)
    wdn_bf_ref[...] = wdn_ref[...].astype(BF16)
    qkv = 3 * SB_WIDTH
    half = SG_WIDTH // 2
    h = _rms(x_ref[...], g_ref[...]).astype(BF16)

    def cols(lo, width):
        return jnp.dot(h, w_ref[:, lo:lo + width], preferred_element_type=F32)

    vacts = []
    for step, part in enumerate((2, 3, 0, 1)):
        lo = qkv + part * half
        act = jax.nn.gelu(cols(lo, half))
        if part < 2:
            o_ref[:, lo:lo + half] = act.astype(o_ref.dtype)
        else:
            vacts.append(act)
        if part == 3:
            vact = jnp.concatenate(vacts, axis=1)
            mu = jnp.mean(vact, axis=-1, keepdims=True)
            vc = vact - mu
            var = jnp.mean(vc * vc, axis=-1, keepdims=True)
            vn = vc * lax.rsqrt(var + EPS) * lng_ref[...] + lnb_ref[...]
            o_ref[:, qkv + SG_WIDTH:] = vn.astype(o_ref.dtype)
        if step < 3:
            p = cols(step * SB_WIDTH, SB_WIDTH)
            if step == 0:
                p = p * Q_SCALE
            o_ref[:, step * SB_WIDTH:(step + 1) * SB_WIDTH] = p.astype(o_ref.dtype)


def _inproj(x2d, g, w_f32, lng, lnb, w_up, w_down, tm):
    n = x2d.shape[0]
    steps = n // tm
    up_rows, dn_rows = w_up.shape[0] // steps, w_down.shape[0] // steps
    assert up_rows * steps == w_up.shape[0] and dn_rows * steps == w_down.shape[0]
    slab = lambda rows, cols: pl.BlockSpec((rows, cols), lambda i: (i, 0))
    const = lambda shape: pl.BlockSpec(shape, lambda i: (0, 0), pipeline_mode=pl.Buffered(1))
    return pl.pallas_call(
        _inproj_kernel,
        grid=(n // tm,),
        in_specs=[
            pl.BlockSpec((tm, D_MODEL), lambda i: (i, 0)),
            const((1, D_MODEL)), const((D_MODEL, IN_WIDTH)),
            const((1, SG_WIDTH)), const((1, SG_WIDTH)),
            slab(up_rows, w_up.shape[1]), slab(dn_rows, w_down.shape[1]),
        ],
        out_specs=[pl.BlockSpec((tm, IN_WIDTH), lambda i: (i, 0)),
                   slab(up_rows, w_up.shape[1]), slab(dn_rows, w_down.shape[1])],
        out_shape=[jax.ShapeDtypeStruct((n, IN_WIDTH), BF16),
                   jax.ShapeDtypeStruct(w_up.shape, BF16),
                   jax.ShapeDtypeStruct(w_down.shape, BF16)],
        scratch_shapes=[pltpu.VMEM((D_MODEL, IN_WIDTH), BF16)],
        compiler_params=pltpu.CompilerParams(
            dimension_semantics=("arbitrary",), vmem_limit_bytes=VMEM_LIMIT_BYTES),
        name="inproj",
    )(x2d, g, w_f32, lng, lnb, w_up, w_down)


MASKED = -1e30
ROW_CHUNK = 32


def _attn_kernel(q_ref, k_ref, v_ref, o_ref, qm_ref, bias_ref, u_ref, xs_ref, z_ref, a_ref,
                 carry_ref, acc_ref):
    s_len = q_ref.shape[1]
    nsup = s_len // SUPER
    blk = ATT_BLOCK

    @pl.when((pl.program_id(0) == 0) & (pl.program_id(1) == 0))
    def _():
        drow = lax.broadcasted_iota(jnp.int32, (ITEM_ROWS, SUPER), 0)
        dcol = lax.broadcasted_iota(jnp.int32, (ITEM_ROWS, SUPER), 1)
        qpos = ((drow >> 8) << 7) + (drow & (blk - 1))
        bias_ref[...] = jnp.where(dcol < qpos, 0.0, MASKED).astype(F32)
        krow = lax.broadcasted_iota(jnp.int32, (SUPER, SUPER), 0)
        kcol = lax.broadcasted_iota(jnp.int32, (SUPER, SUPER), 1)
        u_ref[...] = jnp.where(krow >= kcol, -1.0, 0.0).astype(BF16)

    lane = lax.broadcasted_iota(jnp.int32, (blk, LANES), 1)
    first_head = lane < SB_HEAD_DIM
    for m in range(nsup):
        for half in range(2):
            qb = q_ref[0, (2 * m + half) * blk:(2 * m + half + 1) * blk, :]
            zero = jnp.zeros_like(qb)
            qm_ref[m, (2 * half) * blk:(2 * half + 1) * blk, :] = jnp.where(first_head, qb, zero)
            qm_ref[m, (2 * half + 1) * blk:(2 * half + 2) * blk, :] = jnp.where(first_head, zero, qb)
    def key_rows(c):
        return slice(c * SUPER, (c + 1) * SUPER)

    def scores(item, slot, hf):
        m, c = item
        per_half = SUPER // ROW_CHUNK
        zhalf = lax.dot_general(qm_ref[m, hf * SUPER:(hf + 1) * SUPER, :], k_ref[0, key_rows(c), :],
                                (((1,), (1,)), ((), ())), preferred_element_type=F32)
        for rc in range(hf * per_half, (hf + 1) * per_half):
            rows = slice(rc * ROW_CHUNK, (rc + 1) * ROW_CHUNK)
            zrows = slice((rc - hf * per_half) * ROW_CHUNK, (rc - hf * per_half + 1) * ROW_CHUNK)
            live = blk if (m == c and rc < SUPER // ROW_CHUNK) else SUPER
            z2 = zhalf[zrows, :live]
            if m == c:
                z2 = z2 + bias_ref[rows, :live]
            sp = jnp.maximum(z2, 0.0) + jnp.log2(1.0 + jnp.exp2(-jnp.abs(z2)))
            xs_ref[slot, rows, :live] = sp.astype(BF16)
            total = jnp.sum(sp, axis=-1, keepdims=True)
            if m == c:
                z_ref[slot, rows, :live] = z2
                carry_ref[m, rows, :] = jnp.broadcast_to(-total, (ROW_CHUNK, blk))
            else:
                carry = carry_ref[m, rows, :]
                z_ref[slot, rows, :live] = z2 + jnp.concatenate([carry, carry], axis=1)
                carry_ref[m, rows, :] = carry - total
            if live < SUPER:
                xs_ref[slot, rows, live:] = jnp.zeros((ROW_CHUNK, SUPER - live), BF16)
                z_ref[slot, rows, live:] = jnp.full((ROW_CHUNK, SUPER - live), MASKED, F32)

    def weights(slot_in, slot_out):
        st = jnp.dot(xs_ref[slot_in], u_ref[...], preferred_element_type=F32)
        for rc in range(ITEM_ROWS // ROW_CHUNK):
            rows = slice(rc * ROW_CHUNK, (rc + 1) * ROW_CHUNK)
            a_ref[slot_out, rows, :] = jnp.exp2(z_ref[slot_in, rows, :] + st[rows]).astype(BF16)

    def values(item, slot):
        m, c = item
        av = jnp.dot(a_ref[slot], v_ref[0, key_rows(c), :], preferred_element_type=F32)
        if m == c:
            acc_ref[m] = av
        else:
            acc_ref[m] += av

    items = [(m, c) for m in range(nsup) for c in range(m, -1, -1)]
    for t in range(len(items) + 2):
        cur, prev = t % 2, (t + 1) % 2
        if t < len(items):
            scores(items[t], cur, 0)
        if 0 <= t - 2 < len(items):
            values(items[t - 2], prev)
        if t < len(items):
            scores(items[t], cur, 1)
        if 0 <= t - 1 < len(items):
            weights(prev, cur)

    for m in range(nsup):
        for half in range(2):
            h0 = acc_ref[m, (2 * half) * blk:(2 * half + 1) * blk, :]
            h1 = acc_ref[m, (2 * half + 1) * blk:(2 * half + 2) * blk, :]
            o_ref[0, (2 * m + half) * blk:(2 * m + half + 1) * blk, :] = jnp.where(first_head, h0, h1)


def _attention(proj3d):
    b, s, _ = proj3d.shape
    nsup = s // SUPER
    nslot = 2
    spec = lambda off: pl.BlockSpec((1, s, LANES), lambda bi, p: (bi, 0, off + p))
    return pl.pallas_call(
        _attn_kernel,
        grid=(b, HEAD_PAIRS),
        in_specs=[spec(0), spec(HEAD_PAIRS), spec(2 * HEAD_PAIRS)],
        out_specs=pl.BlockSpec((1, s, LANES), lambda bi, p: (bi, 0, p)),
        out_shape=jax.ShapeDtypeStruct((b, s, SB_WIDTH), F32),
        scratch_shapes=[
            pltpu.VMEM((nsup, ITEM_ROWS, LANES), BF16),
            pltpu.VMEM((ITEM_ROWS, SUPER), F32),
            pltpu.VMEM((SUPER, SUPER), BF16),
            pltpu.VMEM((nslot, ITEM_ROWS, SUPER), BF16),
            pltpu.VMEM((nslot, ITEM_ROWS, SUPER), F32),
            pltpu.VMEM((nslot, ITEM_ROWS, SUPER), BF16),
            pltpu.VMEM((nsup, ITEM_ROWS, ATT_BLOCK), F32),
            pltpu.VMEM((nsup, ITEM_ROWS, LANES), F32),
        ],
        compiler_params=pltpu.CompilerParams(
            dimension_semantics=("arbitrary", "arbitrary"), vmem_limit_bytes=VMEM_LIMIT_BYTES),
        name="stickbreak_attn",
    )(proj3d, proj3d, proj3d)


def _mix_mlp_kernel(x_ref, a_ref, u_ref, vn_ref, sgw_ref, sgb_ref,
                    ong_ref, woutf32_ref, nmg_ref, wup_ref, wdn_ref, nfg_ref, o_ref,
                    mixed_ref, wout_ref):
    _cast_once(woutf32_ref, wout_ref, 2 * LANES)
    tm = x_ref.shape[0]
    vn = vn_ref[...]

    pos_t = lax.broadcasted_iota(jnp.int32, (SG_BLOCK, SG_BLOCK), 0)
    pos_s = lax.broadcasted_iota(jnp.int32, (SG_BLOCK, SG_BLOCK), 1)
    chunk_mask = (pos_s // CHUNK) <= (pos_t // CHUNK)
    lane = lax.broadcasted_iota(jnp.int32, (SG_BLOCK, LANES), 1)
    first = lane < SG_GROUP_DIM
    zero = jnp.zeros((SG_BLOCK, LANES), BF16)
    for pair in range(SG_GROUPS // 2):
        w0 = jnp.where(chunk_mask, sgw_ref[2 * pair], 0.0).astype(BF16)
        w1 = jnp.where(chunk_mask, sgw_ref[2 * pair + 1], 0.0).astype(BF16)
        w_pair = jnp.concatenate([w0, w1], axis=1)
        for r in range(tm // SG_BLOCK):
            vblk = vn[r * SG_BLOCK:(r + 1) * SG_BLOCK, pair * LANES:(pair + 1) * LANES]
            rhs = jnp.concatenate([jnp.where(first, vblk, zero), jnp.where(first, zero, vblk)], axis=0)
            mixed_ref[r * SG_BLOCK:(r + 1) * SG_BLOCK, pair * LANES:(pair + 1) * LANES] = (
                jnp.dot(w_pair, rhs, preferred_element_type=F32)
                + sgb_ref[:, pair * LANES:(pair + 1) * LANES])
    b_out = u_ref[...].astype(F32) * mixed_ref[...]

    ong = ong_ref[...]
    a_n = _rms(a_ref[...], ong[:, :SB_WIDTH]).astype(BF16)
    b_n = _rms(b_out, ong[:, SB_WIDTH:]).astype(BF16)
    mix = jnp.concatenate([a_n, b_n], axis=1)
    x1 = x_ref[...] + jnp.dot(mix, wout_ref[...], preferred_element_type=F32)

    h = _rms(x1, nmg_ref[...]).astype(BF16)
    y = x1
    fc = D_MODEL
    for c in range(D_FF // fc):
        up = jnp.dot(h, wup_ref[:, c * fc:(c + 1) * fc], preferred_element_type=F32)
        act = jnp.square(jnp.maximum(up, 0.0)).astype(BF16)
        y = y + jnp.dot(act, wdn_ref[c * fc:(c + 1) * fc, :], preferred_element_type=F32)
    o_ref[...] = _rms(y, nfg_ref[...])


def _mix_mlp(x2d, a2d, proj2d, sgw, sgb_full, ong, wout, nmg, wup, wdn, nfg, tm):
    n = x2d.shape[0]
    const = lambda shape: pl.BlockSpec(shape, lambda i: (0,) * len(shape),
                                       pipeline_mode=pl.Buffered(1))
    u_blk = 3 * SB_WIDTH // SG_WIDTH
    return pl.pallas_call(
        _mix_mlp_kernel,
        grid=(n // tm,),
        in_specs=[
            pl.BlockSpec((tm, D_MODEL), lambda i: (i, 0)),
            pl.BlockSpec((tm, SB_WIDTH), lambda i: (i, 0)),
            pl.BlockSpec((tm, SG_WIDTH), lambda i: (i, u_blk)),
            pl.BlockSpec((tm, SG_WIDTH), lambda i: (i, u_blk + 1)),
            const((SG_GROUPS, SG_BLOCK, SG_BLOCK)), const((SG_BLOCK, SG_WIDTH)),
            const((1, MIX_WIDTH)), const((MIX_WIDTH, D_MODEL)),
            const((1, D_MODEL)), const((D_MODEL, D_FF)), const((D_FF, D_MODEL)),
            const((1, D_MODEL)),
        ],
        out_specs=pl.BlockSpec((tm, D_MODEL), lambda i: (i, 0)),
        out_shape=jax.ShapeDtypeStruct((n, D_MODEL), F32),
        scratch_shapes=[pltpu.VMEM((tm, SG_WIDTH), F32), pltpu.VMEM((MIX_WIDTH, D_MODEL), BF16)],
        compiler_params=pltpu.CompilerParams(
            dimension_semantics=("arbitrary",), vmem_limit_bytes=VMEM_LIMIT_BYTES),
        name="mix_mlp",
    )(x2d, a2d, proj2d, proj2d, sgw, sgb_full, ong, wout, nmg, wup, wdn, nfg)


def kernel(x, norm_mix_g, w_in, sg_ln_g, sg_ln_b, sg_w, sg_b, out_norm_g, w_out,
           norm_mlp_g, w_up, w_down, norm_final_g):
    b, s, d = x.shape
    n = b * s
    assert w_in.shape[0] == 1, "the final norm is fused into the (single) layer's last kernel"
    layer = 0
    x2d = x.reshape(n, d)
    proj, w_up_bf, w_down_bf = _inproj(x2d, norm_mix_g[layer][None], w_in[layer],
                                       sg_ln_g[layer][None], sg_ln_b[layer][None],
                                       w_up[layer], w_down[layer], tm=512)
    a_out = _attention(proj.reshape(b, s, IN_WIDTH))
    sgb_full = jnp.repeat(jnp.transpose(sg_b[layer]), SG_GROUP_DIM, axis=1)
    out = _mix_mlp(
        x2d, a_out.reshape(n, SB_WIDTH), proj, sg_w[layer], sgb_full,
        out_norm_g[layer][None], w_out[layer], norm_mlp_g[layer][None],
        w_up_bf, w_down_bf, norm_final_g[None], tm=512)
    return out.reshape(b, s, d)
```
